```python
import math
import jax, jax.numpy as jnp
from jax import lax
import numpy as np

D_MODEL = 1024
BATCH = 8
SEQ = 2048
DEPTH = 4
DEC_BATCH = 128
DEC_SEQ = 8
PAST_LEN = 2048
PAGE_SIZE = 128

N_MIXERS = 2
N_ATTN_LAYERS = (DEPTH + 1) // 2
N_MLSTM_LAYERS = DEPTH // 2
DA_HEAD_DIM = 64
DA_V_DIM = 2 * DA_HEAD_DIM
DA_HEADS = D_MODEL // DA_V_DIM
ROPE_THETA = 10000.0
Q_BLOCK = 128
ML_HEADS = 8
ML_V_DIM = D_MODEL // ML_HEADS
ML_QK_DIM = ML_V_DIM // 2
ML_CHUNK = 64
GATE_CAP = 15.0
ML_IN = 2 * ML_HEADS * ML_QK_DIM + 2 * D_MODEL + 2 * ML_HEADS
D_FF = ((8 * D_MODEL // 3 + 127) // 128) * 128
CONV_W = 3
EPS = 1e-6

kernel_name = "hybrid_diffattn_mlstm_convffn_step"


def rmsnorm(x, g):
    xf = x.astype(jnp.float32)
    y = xf * lax.rsqrt(jnp.mean(xf * xf, axis=-1, keepdims=True) + EPS)
    return (y * g.astype(jnp.float32)).astype(x.dtype)


def rope(x, pos):
    half = x.shape[-1] // 2
    freqs = ROPE_THETA ** (-jnp.arange(half, dtype=jnp.float32) / half)
    ang = pos.astype(jnp.float32)[:, None] * freqs[None, :]
    shape = (1, x.shape[1]) + (1,) * (x.ndim - 3) + (half,)
    cos = jnp.cos(ang).reshape(shape)
    sin = jnp.sin(ang).reshape(shape)
    xf = x.astype(jnp.float32)
    x1, x2 = xf[..., :half], xf[..., half:]
    return jnp.concatenate([x1 * cos - x2 * sin, x2 * cos + x1 * sin], axis=-1).astype(x.dtype)


def diff_lambda(lam_params, lam_init):
    lp = lam_params.astype(jnp.float32)
    return jnp.exp(jnp.sum(lp[0] * lp[1])) - jnp.exp(jnp.sum(lp[2] * lp[3])) + lam_init


def diff_qkv(h, w_qkv, pos):
    B, L, _ = h.shape
    qk_w = DA_HEADS * 2 * DA_HEAD_DIM
    qkv = h @ w_qkv
    q, k, v = jnp.split(qkv, [qk_w, 2 * qk_w], axis=-1)
    q = rope(q.reshape(B, L, DA_HEADS, 2, DA_HEAD_DIM), pos)
    k = rope(k.reshape(B, L, DA_HEADS, 2, DA_HEAD_DIM), pos)
    v = v.reshape(B, L, DA_HEADS, DA_V_DIM)
    return q, k, v


def diff_attend(q, k, v, q_pos, k_pos, lam):
    s = jnp.einsum('bqhmd,bkhmd->bhmqk', q.astype(jnp.float32), k.astype(jnp.float32)) * (DA_HEAD_DIM ** -0.5)
    mask = k_pos[None, :] <= q_pos[:, None]
    s = jnp.where(mask, s, -jnp.inf)
    p = jax.nn.softmax(s, axis=-1)
    a = p[:, :, 0] - lam * p[:, :, 1]
    return jnp.einsum('bhqk,bkhd->bqhd', a, v.astype(jnp.float32))


def diff_attend_prompt(q, k, v, lam):
    B, L = q.shape[:2]
    nb = L // Q_BLOCK
    qb = jnp.swapaxes(q.reshape(B, nb, Q_BLOCK, DA_HEADS, 2, DA_HEAD_DIM), 0, 1)
    k_pos = jnp.arange(L)

    def blk(args):
        qi, bi = args
        q_pos = bi * Q_BLOCK + jnp.arange(Q_BLOCK)
        return diff_attend(qi, k, v, q_pos, k_pos, lam)

    out = lax.map(blk, (qb, jnp.arange(nb)))
    return jnp.swapaxes(out, 0, 1).reshape(B, L, DA_HEADS, DA_V_DIM)


def diff_out(o, g_subln, lam_init, w_o, dtype):
    B, L = o.shape[:2]
    o = rmsnorm(o, g_subln) * (1.0 - lam_init)
    return o.reshape(B, L, DA_HEADS * DA_V_DIM).astype(dtype) @ w_o


def gather_pages(cache, page_table):
    g = cache[page_table]
    return g.reshape(g.shape[0], g.shape[1] * g.shape[2], *g.shape[3:])


def mlstm_chunkwise(q, k, v, li, lf, C0, n0, m0):
    B, L = q.shape[:2]
    c = min(ML_CHUNK, L)
    nc = L // c
    f32 = jnp.float32

    def to_chunks(a):
        return jnp.swapaxes(a.astype(f32).reshape(B, nc, c, *a.shape[2:]), 0, 1)

    qs = to_chunks(q) * (ML_QK_DIM ** -0.5)
    ks, vs, lis, lfs = to_chunks(k), to_chunks(v), to_chunks(li), to_chunks(lf)
    causal = jnp.tril(jnp.ones((c, c), dtype=bool))

    def step(carry, inp):
        C, n, m = carry
        qc, kc, vc, lic, lfc = inp
        bh = jnp.swapaxes(jnp.cumsum(lfc, axis=1), 1, 2)
        lih = jnp.swapaxes(lic, 1, 2)
        D = bh[:, :, :, None] - bh[:, :, None, :] + lih[:, :, None, :]
        D = jnp.where(causal, D, -jnp.inf)
        inter = bh + m[:, :, None]
        mt = jnp.maximum(inter, jnp.max(D, axis=-1))
        w_intra = jnp.exp(D - mt[..., None])
        w_inter = jnp.exp(inter - mt)
        sw = w_intra * jnp.einsum('bthd,bshd->bhts', qc, kc)
        num = (jnp.einsum('bhts,bshv->bthv', sw, vc)
               + jnp.einsum('bthd,bhdv->bthv', qc, C) * jnp.swapaxes(w_inter, 1, 2)[..., None])
        den = jnp.sum(sw, axis=-1) + w_inter * jnp.einsum('bthd,bhd->bht', qc, n)
        denom = jnp.maximum(jnp.abs(den), jnp.exp(-mt))
        h = num / jnp.swapaxes(denom, 1, 2)[..., None]
        b_last = bh[:, :, -1]
        m_new = mt[:, :, -1]
        decay = jnp.exp(b_last + m - m_new)
        w_state = jnp.exp(b_last[:, :, None] - bh + lih - m_new[:, :, None])
        C_new = decay[..., None, None] * C + jnp.einsum('bhs,bshd,bshv->bhdv', w_state, kc, vc)
        n_new = decay[..., None] * n + jnp.einsum('bhs,bshd->bhd', w_state, kc)
        return (C_new, n_new, m_new), h

    (C, n, m), hs = lax.scan(step, (C0.astype(f32), n0.astype(f32), m0.astype(f32)), (qs, ks, vs, lis, lfs))
    h = jnp.swapaxes(hs, 0, 1).reshape(B, L, ML_HEADS, ML_V_DIM)
    return h, C, n, m


def mlstm_layer(h, w_in, b_gate, g_norm, w_out, C0, n0, m0):
    B, L, _ = h.shape
    qk_w = ML_HEADS * ML_QK_DIM
    proj = h @ w_in
    q, k, v, o, gates = jnp.split(proj, [qk_w, 2 * qk_w, 2 * qk_w + D_MODEL, 2 * qk_w + 2 * D_MODEL], axis=-1)
    q = q.reshape(B, L, ML_HEADS, ML_QK_DIM)
    k = k.reshape(B, L, ML_HEADS, ML_QK_DIM)
    v = v.reshape(B, L, ML_HEADS, ML_V_DIM)
    pre = gates.astype(jnp.float32) + b_gate.astype(jnp.float32)
    pre = GATE_CAP * jnp.tanh(pre / GATE_CAP)
    li = pre[..., :ML_HEADS]
    lf = jax.nn.log_sigmoid(pre[..., ML_HEADS:])
    hh, C, n, m = mlstm_chunkwise(q, k, v, li, lf, C0, n0, m0)
    hn = rmsnorm(hh, g_norm.reshape(ML_HEADS, ML_V_DIM)).reshape(B, L, D_MODEL)
    out = (jax.nn.sigmoid(o.astype(jnp.float32)) * hn).astype(h.dtype) @ w_out
    return out, C, n, m


def conv_ffn(h, w_up, conv_w, conv_b, w_down, conv_state):
    L = h.shape[1]
    up = h @ w_up
    g, u = jnp.split(up, [D_FF], axis=-1)
    gp = jnp.concatenate([conv_state.astype(g.dtype), g], axis=1)
    gc = conv_b
    for j in range(CONV_W):
        gc = gc + gp[:, j:j + L] * conv_w[j]
    y = (jax.nn.silu(gc) * u) @ w_down
    return y, gp[:, -(CONV_W - 1):]


def setup_inputs(seed: int = 0) -> dict:
    key = jax.random.key(seed)
    ks = jax.random.split(key, 32)
    f32 = jnp.float32
    n_pages = PAST_LEN // PAGE_SIZE
    n_used = DEC_BATCH * n_pages
    n_pool = n_used + n_used // 4

    def nrm(k, shape, s):
        return jax.random.normal(k, shape, f32) * s

    page_table = jax.random.permutation(ks[0], n_pool)[:n_used].reshape(DEC_BATCH, n_pages).astype(jnp.int32)
    b_gate = jnp.concatenate([
        nrm(ks[18], (N_MLSTM_LAYERS, ML_HEADS), 0.1),
        jnp.linspace(3.0, 6.0, ML_HEADS, dtype=f32)[None, :] + nrm(ks[19], (N_MLSTM_LAYERS, ML_HEADS), 0.1),
    ], axis=-1)
    return {
        "x_prompt": nrm(ks[1], (BATCH, SEQ, D_MODEL), 1.0),
        "x_sample": nrm(ks[2], (DEC_BATCH, DEC_SEQ, D_MODEL), 1.0),
        "cache_k": nrm(ks[3], (N_ATTN_LAYERS, n_pool, PAGE_SIZE, DA_HEADS, 2 * DA_HEAD_DIM), 1.0),
        "cache_v": nrm(ks[4], (N_ATTN_LAYERS, n_pool, PAGE_SIZE, DA_HEADS, DA_V_DIM), 1.0),
        "page_table": page_table,
        "state_mlstm_c": nrm(ks[5], (N_MLSTM_LAYERS, DEC_BATCH, ML_HEADS, ML_QK_DIM, ML_V_DIM), 0.5),
        "state_mlstm_n": nrm(ks[6], (N_MLSTM_LAYERS, DEC_BATCH, ML_HEADS, ML_QK_DIM), 0.5),
        "state_mlstm_m": nrm(ks[7], (N_MLSTM_LAYERS, DEC_BATCH, ML_HEADS), 0.5),
        "state_conv": nrm(ks[8], (DEPTH, DEC_BATCH, CONV_W - 1, D_FF), 1.0),
        "g_mix_norm": 1.0 + nrm(ks[9], (DEPTH, D_MODEL), 0.01),
        "g_ffn_norm": 1.0 + nrm(ks[10], (DEPTH, D_MODEL), 0.01),
        "g_final": 1.0 + nrm(ks[11], (D_MODEL,), 0.01),
        "attn_w_qkv": nrm(ks[12], (N_ATTN_LAYERS, D_MODEL, 3 * D_MODEL), D_MODEL ** -0.5),
        "attn_lambda": nrm(ks[13], (N_ATTN_LAYERS, 4, DA_HEAD_DIM), 0.1),
        "attn_subln": 1.0 + nrm(ks[14], (N_ATTN_LAYERS, DA_V_DIM), 0.01),
        "attn_w_o": nrm(ks[15], (N_ATTN_LAYERS, D_MODEL, D_MODEL), D_MODEL ** -0.5),
        "ml_w_in": nrm(ks[16], (N_MLSTM_LAYERS, D_MODEL, ML_IN), D_MODEL ** -0.5),
        "ml_b_gate": b_gate,
        "ml_norm": 1.0 + nrm(ks[20], (N_MLSTM_LAYERS, D_MODEL), 0.01),
        "ml_w_out": nrm(ks[21], (N_MLSTM_LAYERS, D_MODEL, D_MODEL), D_MODEL ** -0.5),
        "ffn_w_up": nrm(ks[22], (DEPTH, D_MODEL, 2 * D_FF), D_MODEL ** -0.5),
        "ffn_conv_w": nrm(ks[23], (DEPTH, CONV_W, D_FF), CONV_W ** -0.5),
        "ffn_conv_b": nrm(ks[24], (DEPTH, D_FF), 0.01),
        "ffn_w_down": nrm(ks[25], (DEPTH, D_FF, D_MODEL), D_FF ** -0.5),
    }


def reference(x_prompt, x_sample, cache_k, cache_v, page_table, state_mlstm_c, state_mlstm_n, state_mlstm_m,
              state_conv, g_mix_norm, g_ffn_norm, g_final, attn_w_qkv, attn_lambda, attn_subln, attn_w_o,
              ml_w_in, ml_b_gate, ml_norm, ml_w_out, ffn_w_up, ffn_conv_w, ffn_conv_b, ffn_w_down):
    B, L = x_prompt.shape[:2]
    DB, DL = x_sample.shape[:2]
    past = page_table.shape[1] * PAGE_SIZE
    pos_p = jnp.arange(L)
    pos_s = past + jnp.arange(DL)
    pos_all = jnp.arange(past + DL)
    xp, xs = x_prompt, x_sample
    k_p, v_p, k_s, v_s = [], [], [], []
    c_p, n_p, m_p, c_s, n_s, m_s = [], [], [], [], [], []
    cv_p, cv_s = [], []
    for i in range(DEPTH):
        j = i // N_MIXERS
        hp = rmsnorm(xp, g_mix_norm[i])
        hs = rmsnorm(xs, g_mix_norm[i])
        if i % N_MIXERS == 0:
            lam_init = 0.8 - 0.6 * math.exp(-0.3 * i)
            lam = diff_lambda(attn_lambda[j], lam_init)
            qp, kp, vp = diff_qkv(hp, attn_w_qkv[j], pos_p)
            op = diff_attend_prompt(qp, kp, vp, lam)
            qs, kss, vss = diff_qkv(hs, attn_w_qkv[j], pos_s)
            k_past = gather_pages(cache_k[j], page_table).reshape(DB, past, DA_HEADS, 2, DA_HEAD_DIM).astype(kss.dtype)
            v_past = gather_pages(cache_v[j], page_table).astype(vss.dtype)
            os_ = diff_attend(qs, jnp.concatenate([k_past, kss], axis=1), jnp.concatenate([v_past, vss], axis=1),
                              pos_s, pos_all, lam)
            xp = xp + diff_out(op, attn_subln[j], lam_init, attn_w_o[j], xp.dtype)
            xs = xs + diff_out(os_, attn_subln[j], lam_init, attn_w_o[j], xs.dtype)
            k_p.append(kp.reshape(B, L, DA_HEADS, 2 * DA_HEAD_DIM))
            v_p.append(vp)
            k_s.append(kss.reshape(DB, DL, DA_HEADS, 2 * DA_HEAD_DIM))
            v_s.append(vss)
        else:
            C0 = jnp.zeros((B, ML_HEADS, ML_QK_DIM, ML_V_DIM), jnp.float32)
            n0 = jnp.zeros((B, ML_HEADS, ML_QK_DIM), jnp.float32)
            m0 = jnp.zeros((B, ML_HEADS), jnp.float32)
            op, Cp, np_, mp = mlstm_layer(hp, ml_w_in[j], ml_b_gate[j], ml_norm[j], ml_w_out[j], C0, n0, m0)
            os_, Cs, ns, ms = mlstm_layer(hs, ml_w_in[j], ml_b_gate[j], ml_norm[j], ml_w_out[j],
                                          state_mlstm_c[j], state_mlstm_n[j], state_mlstm_m[j])
            xp = xp + op
            xs = xs + os_
            c_p.append(Cp)
            n_p.append(np_)
            m_p.append(mp)
            c_s.append(Cs)
            n_s.append(ns)
            m_s.append(ms)
        hp = rmsnorm(xp, g_ffn_norm[i])
        hs = rmsnorm(xs, g_ffn_norm[i])
        zero_buf = jnp.zeros((B, CONV_W - 1, D_FF), xp.dtype)
        yp, bp = conv_ffn(hp, ffn_w_up[i], ffn_conv_w[i], ffn_conv_b[i], ffn_w_down[i], zero_buf)
        ys, bs = conv_ffn(hs, ffn_w_up[i], ffn_conv_w[i], ffn_conv_b[i], ffn_w_down[i], state_conv[i])
        xp = xp + yp
        xs = xs + ys
        cv_p.append(bp)
        cv_s.append(bs)
    y_prompt = rmsnorm(xp, g_final)
    y_sample = rmsnorm(xs, g_final)
    return (y_prompt, y_sample,
            jnp.stack(k_p), jnp.stack(v_p), jnp.stack(k_s), jnp.stack(v_s),
            jnp.stack(c_p), jnp.stack(n_p), jnp.stack(m_p),
            jnp.stack(c_s), jnp.stack(n_s), jnp.stack(m_s),
            jnp.stack(cv_p), jnp.stack(cv_s))
```

```python
import functools
import math

import jax
import jax.numpy as jnp
from jax import lax
from jax.experimental import pallas as pl
from jax.experimental.pallas import tpu as pltpu

F32 = jnp.float32
BF16 = jnp.bfloat16

LANES = 128
SUBLANES = 8
VMEM_LIMIT = 56 * 1024 * 1024

EPS = 1e-6
ROPE_THETA = 10000.0
PAGE_SIZE = 128
N_HEADS = 8
HEAD_W = 128
DA_HEAD_DIM = 64
ML_QK_DIM = 64
GATE_CAP = 15.0
CONV_W = 3
NEG = -1e30


def _cparams(sem):
    return pltpu.CompilerParams(dimension_semantics=sem, vmem_limit_bytes=VMEM_LIMIT)


def _rmsnorm_rows(x, g):
    return x * lax.rsqrt(jnp.mean(x * x, axis=-1, keepdims=True) + EPS) * g


def _rope_table_kernel(tp_ref, ts_ref, *, past, dec_seq):
    def tables(pos, shape):
        lane = lax.broadcasted_iota(jnp.int32, shape, 1)
        half = DA_HEAD_DIM // 2
        fidx = (lane & (half - 1)).astype(F32)
        freq = jnp.exp(fidx * (-math.log(ROPE_THETA) / half))
        ang = pos * freq
        c = jnp.cos(ang)
        s = jnp.sin(ang)
        first = (lane & (DA_HEAD_DIM - 1)) < half
        return c, jnp.where(first, -s, 0.0), jnp.where(first, 0.0, s)

    shp = tp_ref.shape[1:]
    pos_p = lax.broadcasted_iota(jnp.int32, shp, 0).astype(F32)
    c, s1, s2 = tables(pos_p, shp)
    tp_ref[0] = c
    tp_ref[1] = s1
    tp_ref[2] = s2
    shs = ts_ref.shape[1:]
    row = lax.broadcasted_iota(jnp.int32, shs, 0)
    pos_s = (past + (row & (dec_seq - 1))).astype(F32)
    c, s1, s2 = tables(pos_s, shs)
    ts_ref[0] = c
    ts_ref[1] = s1
    ts_ref[2] = s2


def rope_tables(seq, past, dec_seq, sample_rows):
    return pl.pallas_call(
        functools.partial(_rope_table_kernel, past=past, dec_seq=dec_seq),
        out_shape=(jax.ShapeDtypeStruct((3, seq, LANES), F32),
                   jax.ShapeDtypeStruct((3, sample_rows, LANES), F32)),
        name="rope_tables",
    )()


def _norm_into(hn_ref, x_ref, g_ref):
    hn_ref[...] = _rmsnorm_rows(x_ref[...], g_ref[...]).astype(BF16)


def _rope_head(y, c, s1, s2):
    return y * c + pltpu.roll(y, LANES - 32, 1) * s1 + pltpu.roll(y, 32, 1) * s2


def _qkv_kernel(x_ref, g_ref, w_ref, rope_ref, q_ref, k_ref, v_ref, hn_ref):
    j = pl.program_id(1)

    @pl.when(j == 0)
    def _():
        _norm_into(hn_ref, x_ref, g_ref)

    y = jnp.dot(hn_ref[...], w_ref[...], preferred_element_type=F32)

    @pl.when(j == 0)
    def _():
        c, s1, s2 = rope_ref[0], rope_ref[1], rope_ref[2]
        for h in range(N_HEADS):
            sl = slice(h * HEAD_W, (h + 1) * HEAD_W)
            q_ref[:, sl] = (_rope_head(y[:, sl], c, s1, s2) * (DA_HEAD_DIM ** -0.5)).astype(BF16)

    @pl.when(j == 1)
    def _():
        c, s1, s2 = rope_ref[0], rope_ref[1], rope_ref[2]
        for h in range(N_HEADS):
            sl = slice(h * HEAD_W, (h + 1) * HEAD_W)
            k_ref[:, sl] = _rope_head(y[:, sl], c, s1, s2)

    @pl.when(j == 2)
    def _():
        v_ref[...] = y


def qkv_proj(x, g, w_bf, rope_tab, tm, rope_blocks):
    T, D = x.shape
    n_i = T // tm

    def rope_idx(i, j):
        return (0, i % rope_blocks, 0)

    return pl.pallas_call(
        _qkv_kernel,
        grid=(n_i, 3),
        in_specs=[
            pl.BlockSpec((tm, D), lambda i, j: (i, 0)),
            pl.BlockSpec((1, D), lambda i, j: (0, 0)),
            pl.BlockSpec((D, D), lambda i, j: (0, j)),
            pl.BlockSpec((3, tm, LANES), rope_idx),
        ],
        out_specs=[
            pl.BlockSpec((tm, D), lambda i, j: (i, 0)),
            pl.BlockSpec((tm, D), lambda i, j: (i, 0)),
            pl.BlockSpec((tm, D), lambda i, j: (i, 0)),
        ],
        out_shape=(jax.ShapeDtypeStruct((T, D), BF16),
                   jax.ShapeDtypeStruct((T, D), F32),
                   jax.ShapeDtypeStruct((T, D), F32)),
        scratch_shapes=[pltpu.VMEM((tm, D), BF16)],
        compiler_params=_cparams(("arbitrary", "arbitrary")),
        name="qkv_proj",
    )(x, g, w_bf, rope_tab)


def _mlin_kernel(x_ref, g_ref, w_ref, wg_ref, qk_ref, v_ref, o_ref, gt_ref, hn_ref):
    j = pl.program_id(1)

    @pl.when(j == 0)
    def _():
        _norm_into(hn_ref, x_ref, g_ref)
        gt_ref[...] = jnp.dot(hn_ref[...], wg_ref[...], preferred_element_type=F32)

    y = jnp.dot(hn_ref[...], w_ref[...], preferred_element_type=F32)

    @pl.when(j == 0)
    def _():
        half = y.shape[1] // 2
        qk_ref[:, :half] = (y[:, :half] * (ML_QK_DIM ** -0.5)).astype(BF16)
        qk_ref[:, half:] = y[:, half:].astype(BF16)

    @pl.when(j == 1)
    def _():
        v_ref[...] = y.astype(BF16)

    @pl.when(j == 2)
    def _():
        o_ref[...] = y


def mlstm_in_proj(x, g, w_bf, wg_bf, tm):
    T, D = x.shape
    GW = wg_bf.shape[1]
    return pl.pallas_call(
        _mlin_kernel,
        grid=(T // tm, 3),
        in_specs=[
            pl.BlockSpec((tm, D), lambda i, j: (i, 0)),
            pl.BlockSpec((1, D), lambda i, j: (0, 0)),
            pl.BlockSpec((D, D), lambda i, j: (0, j)),
            pl.BlockSpec((D, GW), lambda i, j: (0, 0)),
        ],
        out_specs=[
            pl.BlockSpec((tm, D), lambda i, j: (i, 0)),
            pl.BlockSpec((tm, D), lambda i, j: (i, 0)),
            pl.BlockSpec((tm, D), lambda i, j: (i, 0)),
            pl.BlockSpec((tm, GW), lambda i, j: (i, 0)),
        ],
        out_shape=(jax.ShapeDtypeStruct((T, D), BF16),
                   jax.ShapeDtypeStruct((T, D), BF16),
                   jax.ShapeDtypeStruct((T, D), F32),
                   jax.ShapeDtypeStruct((T, GW), F32)),
        scratch_shapes=[pltpu.VMEM((tm, D), BF16)],
        compiler_params=_cparams(("arbitrary", "arbitrary")),
        name="mlstm_in_proj",
    )(x, g, w_bf, wg_bf)


def _out_proj_kernel(x_ref, a_ref, w_ref, o_ref):
    o_ref[...] = x_ref[...] + jnp.dot(a_ref[...], w_ref[...], preferred_element_type=F32)


def out_proj(x, a_bf, w_bf, tm):
    T, D = x.shape
    return pl.pallas_call(
        _out_proj_kernel,
        grid=(T // tm,),
        in_specs=[
            pl.BlockSpec((tm, D), lambda i: (i, 0)),
            pl.BlockSpec((tm, D), lambda i: (i, 0)),
            pl.BlockSpec((D, D), lambda i: (0, 0)),
        ],
        out_specs=pl.BlockSpec((tm, D), lambda i: (i, 0)),
        out_shape=jax.ShapeDtypeStruct((T, D), F32),
        compiler_params=_cparams(("arbitrary",)),
        name="out_proj",
    )(x, a_bf, w_bf)


def _diff_lambda(lam_ref, lam_init):
    lp = lam_ref[...]
    a = jnp.sum(lp[0:1] * lp[1:2], axis=-1, keepdims=True)
    b = jnp.sum(lp[2:3] * lp[3:4], axis=-1, keepdims=True)
    return jnp.exp(a) - jnp.exp(b) + lam_init


def _split_maps(q):
    qf = q.astype(F32)
    lane = lax.broadcasted_iota(jnp.int32, qf.shape, 1)
    return (jnp.where(lane < DA_HEAD_DIM, qf, 0.0).astype(BF16),
            jnp.where(lane >= DA_HEAD_DIM, qf, 0.0).astype(BF16))


def _softmax_update(s, v_bf, m_ref, l_ref, acc_ref, idx):
    m_old = m_ref[idx]
    m_new = jnp.maximum(m_old, jnp.max(s, axis=-1, keepdims=True))
    alpha = jnp.exp(m_old - m_new)
    p = jnp.exp(s - m_new)
    l_ref[idx] = alpha * l_ref[idx] + jnp.sum(p, axis=-1, keepdims=True)
    acc_ref[idx] = alpha * acc_ref[idx] + jnp.dot(p.astype(BF16), v_bf, preferred_element_type=F32)
    m_ref[idx] = m_new


def _diff_finish(o1, o2, lam, g, lam_init):
    o = o1 - lam * o2
    return (_rmsnorm_rows(o, g) * (1.0 - lam_init)).astype(BF16)


_NT = (((1,), (1,)), ((), ()))


def _flash_kernel(lam_ref, g_ref, q_ref, k_ref, v_ref, o_ref, kb, vb, m_s, l_s, acc_s, *, tq, lam_init):
    qi = pl.program_id(2)

    @pl.when(qi == 0)
    def _():
        kb[...] = k_ref[...].astype(BF16)
        vb[...] = v_ref[...].astype(BF16)

    qm = _split_maps(q_ref[...])
    m_s[...] = jnp.full(m_s.shape, NEG, F32)
    l_s[...] = jnp.zeros(l_s.shape, F32)
    acc_s[...] = jnp.zeros(acc_s.shape, F32)

    def step(ki, masked):
        off = pl.multiple_of(ki * tq, tq)
        kblk = kb[pl.ds(off, tq), :]
        vblk = vb[pl.ds(off, tq), :]
        for mp in range(2):
            s = lax.dot_general(qm[mp], kblk, _NT, preferred_element_type=F32)
            if masked:
                row = lax.broadcasted_iota(jnp.int32, s.shape, 0)
                col = lax.broadcasted_iota(jnp.int32, s.shape, 1)
                s = jnp.where(col <= row, s, NEG)
            _softmax_update(s, vblk, m_s, l_s, acc_s, mp)

    def body(ki, carry):
        step(ki, False)
        return carry

    lax.fori_loop(0, qi, body, 0)
    step(qi, True)

    lam = _diff_lambda(lam_ref, lam_init)
    o_ref[...] = _diff_finish(acc_s[0] / l_s[0], acc_s[1] / l_s[1], lam, g_ref[...], lam_init)


def flash_diff_attention(q_bf, k_f32, v_f32, lam_p, g_subln, batch, seq, lam_init, tq):
    T, D = q_bf.shape
    nq = seq // tq
    kern = functools.partial(_flash_kernel, tq=tq, lam_init=lam_init)
    return pl.pallas_call(
        kern,
        grid=(batch, N_HEADS, nq),
        in_specs=[
            pl.BlockSpec(lam_p.shape, lambda b, h, i: (0, 0)),
            pl.BlockSpec((1, HEAD_W), lambda b, h, i: (0, 0)),
            pl.BlockSpec((tq, HEAD_W), lambda b, h, i: (b * nq + i, h)),
            pl.BlockSpec((seq, HEAD_W), lambda b, h, i: (b, h)),
            pl.BlockSpec((seq, HEAD_W), lambda b, h, i: (b, h)),
        ],
        out_specs=pl.BlockSpec((tq, HEAD_W), lambda b, h, i: (b * nq + i, h)),
        out_shape=jax.ShapeDtypeStruct((T, D), BF16),
        scratch_shapes=[
            pltpu.VMEM((seq, HEAD_W), BF16),
            pltpu.VMEM((seq, HEAD_W), BF16),
            pltpu.VMEM((2, tq, 1), F32),
            pltpu.VMEM((2, tq, 1), F32),
            pltpu.VMEM((2, tq, HEAD_W), F32),
        ],
        compiler_params=_cparams(("arbitrary", "arbitrary", "arbitrary")),
        name="flash_diff_attention",
    )(lam_p, g_subln, q_bf, k_f32, v_f32)


def _paged_kernel(pt_ref, lam_ref, g_ref, q_ref, kn_ref, vn_ref, *rest, n_pg, lam_init):
    del pt_ref
    k_refs = rest[:n_pg]
    v_refs = rest[n_pg:2 * n_pg]
    o_ref = rest[2 * n_pg]
    qbd, bias, m_s, l_s, acc_s = rest[2 * n_pg + 1:]
    pg = pl.program_id(1)
    rows = q_ref.shape[1]

    @pl.when(pg == 0)
    def _():
        q1, q2 = _split_maps(q_ref[0])
        qbd[0:rows] = q1
        qbd[rows:2 * rows] = q2
        r = lax.broadcasted_iota(jnp.int32, bias.shape, 0)
        c = lax.broadcasted_iota(jnp.int32, bias.shape, 1)
        bias[...] = jnp.where((r & (N_HEADS - 1)) == (c & (N_HEADS - 1)), 0.0, NEG)
        m_s[...] = jnp.full(m_s.shape, NEG, F32)
        l_s[...] = jnp.zeros(l_s.shape, F32)
        acc_s[...] = jnp.zeros(acc_s.shape, F32)

    for r in range(n_pg):
        kp = k_refs[r][0].astype(BF16)
        vp = v_refs[r][0].astype(BF16)
        s = lax.dot_general(qbd[...], kp, _NT, preferred_element_type=F32) + bias[...]
        _softmax_update(s, vp, m_s, l_s, acc_s, 0)

    @pl.when(pg == pl.num_programs(1) - 1)
    def _():
        kn = kn_ref[0].astype(BF16)
        vn = vn_ref[0].astype(BF16)
        s = lax.dot_general(qbd[...], kn, _NT, preferred_element_type=F32)
        rr = lax.broadcasted_iota(jnp.int32, s.shape, 0)
        cc = lax.broadcasted_iota(jnp.int32, s.shape, 1)
        same_head = (rr & (N_HEADS - 1)) == (cc & (N_HEADS - 1))
        q_tok = (rr & (rows - 1)) >> 3
        k_tok = cc >> 3
        s = jnp.where(same_head, jnp.where(k_tok <= q_tok, s, NEG), NEG)
        _softmax_update(s, vn, m_s, l_s, acc_s, 0)
        o = acc_s[0] / l_s[0]
        lam = _diff_lambda(lam_ref, lam_init)
        o_ref[0] = _diff_finish(o[0:rows], o[rows:2 * rows], lam, g_ref[...], lam_init)


def paged_diff_attention(q_bf, k_new, v_new, cache_k, cache_v, page_table, pool_offset, lam_p, g_subln,
                         lam_init, n_pg):
    DB, rows, _ = q_bf.shape
    n_pages = page_table.shape[1]
    page_rows = cache_k.shape[1]
    steps = n_pages // n_pg

    def page_idx(r):
        def idx(b, p, pt):
            return (pool_offset + pt[b * n_pages + p * n_pg + r], 0, 0)
        return idx

    seq_idx = lambda b, p, pt: (b, 0, 0)
    const2 = lambda b, p, pt: (0, 0)
    in_specs = [
        pl.BlockSpec(lam_p.shape, const2),
        pl.BlockSpec((1, HEAD_W), const2),
        pl.BlockSpec((1, rows, HEAD_W), seq_idx),
        pl.BlockSpec((1, rows, HEAD_W), seq_idx),
        pl.BlockSpec((1, rows, HEAD_W), seq_idx),
    ]
    in_specs += [pl.BlockSpec((1, page_rows, HEAD_W), page_idx(r)) for r in range(n_pg)]
    in_specs += [pl.BlockSpec((1, page_rows, HEAD_W), page_idx(r)) for r in range(n_pg)]
    grid_spec = pltpu.PrefetchScalarGridSpec(
        num_scalar_prefetch=1,
        grid=(DB, steps),
        in_specs=in_specs,
        out_specs=pl.BlockSpec((1, rows, HEAD_W), seq_idx),
        scratch_shapes=[
            pltpu.VMEM((2 * rows, HEAD_W), BF16),
            pltpu.VMEM((2 * rows, page_rows), F32),
            pltpu.VMEM((1, 2 * rows, 1), F32),
            pltpu.VMEM((1, 2 * rows, 1), F32),
            pltpu.VMEM((1, 2 * rows, HEAD_W), F32),
        ],
    )
    kern = functools.partial(_paged_kernel, n_pg=n_pg, lam_init=lam_init)
    return pl.pallas_call(
        kern,
        grid_spec=grid_spec,
        out_shape=jax.ShapeDtypeStruct((DB, rows, HEAD_W), BF16),
        compiler_params=_cparams(("arbitrary", "arbitrary")),
        name="paged_diff_attention",
    )(page_table.reshape(-1), lam_p, g_subln, q_bf, k_new, v_new,
      *([cache_k] * n_pg), *([cache_v] * n_pg))


def _seg_scan(x, pos, seg, op, ident):
    s = 1
    while s < seg:
        shifted = pltpu.roll(x, s, 0)
        x = op(x, jnp.where(pos >= s, shifted, ident))
        s *= 2
    return x


def _seg_last(x, n_seg, seg):
    r, w = x.shape
    x3 = x.reshape(n_seg, seg, w)
    return jnp.broadcast_to(x3[:, seg - 1:seg, :], (n_seg, seg, w)).reshape(r, w)


def _mlstm_kernel(*refs, n_seg, seg, has_state):
    if has_state:
        (qk_ref, v_ref, o_ref, gt_ref, bg_ref, gn_ref, c_in, n_in, m_in,
         h_ref, c_out, n_out, m_out) = refs
    else:
        qk_ref, v_ref, o_ref, gt_ref, bg_ref, gn_ref, h_ref, c_out, n_out, m_out = refs
        c_in, n_in, m_in = c_out, n_out, m_out

        @pl.when(pl.program_id(1) == 0)
        def _():
            c_out[...] = jnp.zeros(c_out.shape, F32)
            n_out[...] = jnp.zeros(n_out.shape, F32)
            m_out[...] = jnp.zeros(m_out.shape, F32)

    R = n_seg * seg
    D = qk_ref.shape[1]
    KOFF = D // 2
    QK = ML_QK_DIM

    pre = gt_ref[...] + bg_ref[...]
    pre = GATE_CAP * jnp.tanh(pre * (1.0 / GATE_CAP))
    li = pre[:, :LANES]
    xf = pre[:, LANES:]
    lf = jnp.minimum(xf, 0.0) - jnp.log1p(jnp.exp(-jnp.abs(xf)))

    rowi = lax.broadcasted_iota(jnp.int32, (R, LANES), 0)
    pos = rowi & (seg - 1)
    bcum = _seg_scan(lf, pos, seg, jnp.add, 0.0)
    a = li - bcum
    cm = _seg_scan(a, pos, seg, jnp.maximum, NEG)
    m0 = m_in[...]
    mx = jnp.maximum(m0, cm)
    mt = bcum + mx
    u = -mx
    w_inter = jnp.exp(m0 + u)
    emt = jnp.exp(-mt)
    b_last = _seg_last(bcum, n_seg, seg)
    m_new = _seg_last(mt, n_seg, seg)
    decay = jnp.exp(b_last + m0 - m_new)
    w_state = jnp.exp(a + b_last - m_new)
    a_t = a.T

    tr = lax.broadcasted_iota(jnp.int32, (R, R), 0)
    tc = lax.broadcasted_iota(jnp.int32, (R, R), 1)
    if n_seg == 1:
        intra = tc <= tr
    else:
        shift = seg.bit_length() - 1
        intra = ((tr >> shift) == (tc >> shift)) & (tc <= tr)

    lane = lax.broadcasted_iota(jnp.int32, (R, LANES), 1)
    low = lane < QK
    ones_col = jnp.where(lane == 0, 1.0, 0.0).astype(BF16)
    if n_seg > 1:
        colseg = lax.broadcasted_iota(jnp.int32, (QK, R), 1) >> (seg.bit_length() - 1)

    for p in range(N_HEADS // 2):
        ha, hb = 2 * p, 2 * p + 1
        qp = qk_ref[:, p * LANES:(p + 1) * LANES].astype(F32)
        kp_bf = qk_ref[:, KOFF + p * LANES:KOFF + (p + 1) * LANES]
        q_heads = (jnp.where(low, qp, 0.0).astype(BF16), jnp.where(low, 0.0, qp).astype(BF16))
        kw = kp_bf.astype(F32) * jnp.where(low, w_state[:, ha:ha + 1], w_state[:, hb:hb + 1])
        kw_t = kw.T
        sp = []
        for g in range(n_seg):
            cpair = c_in[g, ha:hb + 1].reshape(2 * QK, LANES)
            npair = n_in[g, ha:hb + 1].reshape(2 * QK, LANES)
            sp.append(jnp.concatenate([cpair, npair], axis=1).astype(BF16))
        for hi, h in enumerate((ha, hb)):
            qh = q_heads[hi]
            s = lax.dot_general(qh, kp_bf, _NT, preferred_element_type=F32)
            wi = jnp.where(intra, jnp.exp(u[:, h:h + 1] + a_t[h:h + 1, :]), 0.0)
            sw = wi * s
            vh = v_ref[:, h * LANES:(h + 1) * LANES]
            num = jnp.dot(sw.astype(BF16), vh, preferred_element_type=F32)
            den = jnp.sum(sw, axis=-1, keepdims=True)
            if n_seg == 1:
                inter = jnp.dot(qh, sp[0], preferred_element_type=F32)
            else:
                inter = jnp.concatenate(
                    [jnp.dot(qh[g * seg:(g + 1) * seg], sp[g], preferred_element_type=F32)
                     for g in range(n_seg)], axis=0)
            wcol = w_inter[:, h:h + 1]
            num = num + inter[:, :LANES] * wcol
            den = den + wcol * inter[:, LANES:LANES + 1]
            denom = jnp.maximum(jnp.abs(den), emt[:, h:h + 1])
            hh = num / denom
            hn = _rmsnorm_rows(hh, gn_ref[:, h * LANES:(h + 1) * LANES])
            og = jax.nn.sigmoid(o_ref[:, h * LANES:(h + 1) * LANES])
            h_ref[:, h * LANES:(h + 1) * LANES] = (og * hn).astype(BF16)
            kw_h = kw_t[hi * QK:(hi + 1) * QK, :]
            v_ext = jnp.concatenate([vh, ones_col], axis=1)
            if n_seg == 1:
                upd = jnp.dot(kw_h.astype(BF16), v_ext, preferred_element_type=F32)
                dh = decay[0:1, h:h + 1]
                c_new = dh * c_in[0, h] + upd[:, :LANES]
                n_new = dh * n_in[0, h] + upd[:, LANES:]
                c_out[0, h] = c_new
                n_out[0, h] = n_new
            else:
                lhs = jnp.concatenate(
                    [jnp.where(colseg == g, kw_h, 0.0) for g in range(n_seg)], axis=0).astype(BF16)
                upd = jnp.dot(lhs, v_ext, preferred_element_type=F32)
                for g in range(n_seg):
                    dh = decay[g * seg:g * seg + 1, h:h + 1]
                    ug = upd[g * QK:(g + 1) * QK]
                    c_out[g, h] = dh * c_in[g, h] + ug[:, :LANES]
                    n_out[g, h] = dh * n_in[g, h] + ug[:, LANES:]
    m_out[...] = m_new


def mlstm_prompt(qk, v, o, gates, bgate, gnorm, batch, seq):
    T, D = qk.shape
    R = LANES
    nc = seq // R
    GW = gates.shape[1]
    row = lambda b, c: (b * nc + c, 0)
    const = lambda b, c: (0, 0)
    st = lambda b, c: (b, 0, 0, 0)
    kern = functools.partial(_mlstm_kernel, n_seg=1, seg=R, has_state=False)
    return pl.pallas_call(
        kern,
        grid=(batch, nc),
        in_specs=[
            pl.BlockSpec((R, D), row), pl.BlockSpec((R, D), row), pl.BlockSpec((R, D), row),
            pl.BlockSpec((R, GW), row), pl.BlockSpec((1, GW), const), pl.BlockSpec((1, D), const),
        ],
        out_specs=[
            pl.BlockSpec((R, D), row),
            pl.BlockSpec((1, N_HEADS, ML_QK_DIM, LANES), st),
            pl.BlockSpec((1, N_HEADS, ML_QK_DIM, LANES), st),
            pl.BlockSpec((R, LANES), lambda b, c: (b, 0)),
        ],
        out_shape=(jax.ShapeDtypeStruct((T, D), BF16),
                   jax.ShapeDtypeStruct((batch, N_HEADS, ML_QK_DIM, LANES), F32),
                   jax.ShapeDtypeStruct((batch, N_HEADS, ML_QK_DIM, LANES), F32),
                   jax.ShapeDtypeStruct((batch * R, LANES), F32)),
        compiler_params=_cparams(("arbitrary", "arbitrary")),
        name="mlstm_prompt",
    )(qk, v, o, gates, bgate, gnorm)


def mlstm_sample(qk, v, o, gates, bgate, gnorm, c0, n0_ext, m0_rows, seg):
    T, D = qk.shape
    R = LANES
    n_seg = R // seg
    GW = gates.shape[1]
    row = lambda i: (i, 0)
    const = lambda i: (0, 0)
    st = lambda i: (i, 0, 0, 0)
    st_block = (n_seg, N_HEADS, ML_QK_DIM, LANES)
    kern = functools.partial(_mlstm_kernel, n_seg=n_seg, seg=seg, has_state=True)
    return pl.pallas_call(
        kern,
        grid=(T // R,),
        in_specs=[
            pl.BlockSpec((R, D), row), pl.BlockSpec((R, D), row), pl.BlockSpec((R, D), row),
            pl.BlockSpec((R, GW), row), pl.BlockSpec((1, GW), const), pl.BlockSpec((1, D), const),
            pl.BlockSpec(st_block, st), pl.BlockSpec(st_block, st), pl.BlockSpec((R, LANES), row),
        ],
        out_specs=[
            pl.BlockSpec((R, D), row),
            pl.BlockSpec(st_block, st),
            pl.BlockSpec(st_block, st),
            pl.BlockSpec((R, LANES), row),
        ],
        out_shape=(jax.ShapeDtypeStruct((T, D), BF16),
                   jax.ShapeDtypeStruct(c0.shape, F32),
                   jax.ShapeDtypeStruct(c0.shape, F32),
                   jax.ShapeDtypeStruct((T, LANES), F32)),
        compiler_params=_cparams(("arbitrary",)),
        name="mlstm_sample",
    )(qk, v, o, gates, bgate, gnorm, c0, n0_ext, m0_rows)


def _ffn_core(x_ref, gn_ref, wg_ref, wu_ref, wd_ref, cw_ref, y_ref, hn_ref, shifted, final_g_ref):
    f = pl.program_id(1)

    @pl.when(f == 0)
    def _():
        _norm_into(hn_ref, x_ref, gn_ref)
        y_ref[...] = x_ref[...]

    hn = hn_ref[...]
    g = jnp.dot(hn, wg_ref[...], preferred_element_type=F32)
    u = jnp.dot(hn, wu_ref[...], preferred_element_type=F32)
    g1, g2 = shifted(g)
    cw = cw_ref[...]
    gc = cw[3:4] + cw[0:1] * g2 + cw[1:2] * g1 + cw[2:3] * g
    act = (jax.nn.silu(gc) * u).astype(BF16)
    y_ref[...] += jnp.dot(act, wd_ref[...], preferred_element_type=F32)

    if final_g_ref is not None:
        @pl.when(f == pl.num_programs(1) - 1)
        def _():
            y_ref[...] = _rmsnorm_rows(y_ref[...], final_g_ref[...])
    return g


def _ffn_prompt_kernel(*refs, tiles_per_seq, final_norm):
    if final_norm:
        x_ref, gn_ref, wg_ref, wu_ref, wd_ref, cw_ref, fg_ref, y_ref, st_ref, hn_ref, gbuf, carry = refs
    else:
        x_ref, gn_ref, wg_ref, wu_ref, wd_ref, cw_ref, y_ref, st_ref, hn_ref, gbuf, carry = refs
        fg_ref = None
    i = pl.program_id(0)
    f = pl.program_id(1)
    tm = x_ref.shape[0]

    @pl.when((i == 0) & (f == 0))
    def _():
        carry[...] = jnp.zeros(carry.shape, F32)

    def shifted(g):
        prev = jnp.where(i % tiles_per_seq == 0, 0.0, carry[f])
        gbuf[0:SUBLANES] = prev
        gbuf[SUBLANES:] = g
        return gbuf[SUBLANES - 1:SUBLANES - 1 + tm], gbuf[SUBLANES - 2:SUBLANES - 2 + tm]

    g = _ffn_core(x_ref, gn_ref, wg_ref, wu_ref, wd_ref, cw_ref, y_ref, hn_ref, shifted, fg_ref)
    tail = g[tm - SUBLANES:]
    carry[f] = tail
    st_ref[0] = tail


def _ffn_sample_kernel(*refs, seg, final_norm):
    if final_norm:
        x_ref, gn_ref, wg_ref, wu_ref, wd_ref, cw_ref, e_ref, fg_ref, y_ref, g_ref, hn_ref = refs
    else:
        x_ref, gn_ref, wg_ref, wu_ref, wd_ref, cw_ref, e_ref, y_ref, g_ref, hn_ref = refs
        fg_ref = None
    tm = x_ref.shape[0]

    def shifted(g):
        e = e_ref[...]
        pos = lax.broadcasted_iota(jnp.int32, g.shape, 0) & (seg - 1)
        g1 = jnp.where(pos < 1, pltpu.roll(e, tm - 1, 0), pltpu.roll(g, 1, 0))
        g2 = jnp.where(pos < 2, e, pltpu.roll(g, 2, 0))
        return g1, g2

    g = _ffn_core(x_ref, gn_ref, wg_ref, wu_ref, wd_ref, cw_ref, y_ref, hn_ref, shifted, fg_ref)
    g_ref[...] = g


def _ffn_common_specs(D, tm, tf, n_f):
    return [
        pl.BlockSpec((tm, D), lambda i, f: (i, 0)),
        pl.BlockSpec((1, D), lambda i, f: (0, 0)),
        pl.BlockSpec((D, tf), lambda i, f: (0, f)),
        pl.BlockSpec((D, tf), lambda i, f: (0, n_f + f)),
        pl.BlockSpec((tf, D), lambda i, f: (f, 0)),
        pl.BlockSpec((SUBLANES, tf), lambda i, f: (0, f)),
    ]


def ffn_prompt(x, gn, w_up_bf, w_down_bf, cw, batch, seq, tm, tf, final_g=None):
    T, D = x.shape
    F = w_down_bf.shape[0]
    n_f = F // tf
    tps = seq // tm
    in_specs = _ffn_common_specs(D, tm, tf, n_f)
    args = [x, gn, w_up_bf, w_up_bf, w_down_bf, cw]
    if final_g is not None:
        in_specs.append(pl.BlockSpec((1, D), lambda i, f: (0, 0)))
        args.append(final_g)
    kern = functools.partial(_ffn_prompt_kernel, tiles_per_seq=tps, final_norm=final_g is not None)
    return pl.pallas_call(
        kern,
        grid=(T // tm, n_f),
        in_specs=in_specs,
        out_specs=[
            pl.BlockSpec((tm, D), lambda i, f: (i, 0)),
            pl.BlockSpec((1, SUBLANES, tf), lambda i, f: (i, 0, f)),
        ],
        out_shape=(jax.ShapeDtypeStruct((T, D), F32),
                   jax.ShapeDtypeStruct((T // tm, SUBLANES, F), F32)),
        scratch_shapes=[
            pltpu.VMEM((tm, D), BF16),
            pltpu.VMEM((tm + SUBLANES, tf), F32),
            pltpu.VMEM((n_f, SUBLANES, tf), F32),
        ],
        compiler_params=_cparams(("arbitrary", "arbitrary")),
        name="ffn_prompt",
    )(*args)


def ffn_sample(x, gn, w_up_bf, w_down_bf, cw, e_rows, seg, tm, tf, final_g=None):
    T, D = x.shape
    F = w_down_bf.shape[0]
    n_f = F // tf
    in_specs = _ffn_common_specs(D, tm, tf, n_f)
    in_specs.append(pl.BlockSpec((tm, tf), lambda i, f: (i, f)))
    args = [x, gn, w_up_bf, w_up_bf, w_down_bf, cw, e_rows]
    if final_g is not None:
        in_specs.append(pl.BlockSpec((1, D), lambda i, f: (0, 0)))
        args.append(final_g)
    kern = functools.partial(_ffn_sample_kernel, seg=seg, final_norm=final_g is not None)
    return pl.pallas_call(
        kern,
        grid=(T // tm, n_f),
        in_specs=in_specs,
        out_specs=[
            pl.BlockSpec((tm, D), lambda i, f: (i, 0)),
            pl.BlockSpec((tm, tf), lambda i, f: (i, f)),
        ],
        out_shape=(jax.ShapeDtypeStruct((T, D), F32),
                   jax.ShapeDtypeStruct((T, F), F32)),
        scratch_shapes=[pltpu.VMEM((tm, D), BF16)],
        compiler_params=_cparams(("arbitrary", "arbitrary")),
        name="ffn_sample",
    )(*args)


def kernel(x_prompt, x_sample, cache_k, cache_v, page_table, state_mlstm_c, state_mlstm_n, state_mlstm_m,
           state_conv, g_mix_norm, g_ffn_norm, g_final, attn_w_qkv, attn_lambda, attn_subln, attn_w_o,
           ml_w_in, ml_b_gate, ml_norm, ml_w_out, ffn_w_up, ffn_conv_w, ffn_conv_b, ffn_w_down):
    B, L, D = x_prompt.shape
    DB, DL, _ = x_sample.shape
    depth = g_mix_norm.shape[0]
    n_attn, n_pool = cache_k.shape[:2]
    n_pages = page_table.shape[1]
    past = n_pages * PAGE_SIZE
    d_ff = ffn_w_down.shape[1]
    n_ml_heads = state_mlstm_c.shape[2]
    assert D == N_HEADS * HEAD_W and n_ml_heads == N_HEADS and DL == SUBLANES
    assert cache_k.shape[2:] == (PAGE_SIZE, N_HEADS, HEAD_W) and cache_v.shape[2:] == (PAGE_SIZE, N_HEADS, HEAD_W)

    TP, TS = B * L, DB * DL
    TM = 512
    TM_S = min(TM, TS)
    TM_FFN = 1024
    TM_FFN_S = min(TM_FFN, TS)
    TF = 256
    TQ = 512
    N_PG = 4

    xp = x_prompt.reshape(TP, D)
    xs = x_sample.reshape(TS, D)
    rope_p, rope_s = rope_tables(L, past, DL, TM_S)
    ck = cache_k.reshape(n_attn * n_pool, PAGE_SIZE * N_HEADS, HEAD_W)
    cv = cache_v.reshape(n_attn * n_pool, PAGE_SIZE * N_HEADS, HEAD_W)

    qk_w = N_HEADS * ML_QK_DIM
    k_p, v_p, k_s, v_s = [], [], [], []
    c_p, n_p, m_p, c_s, n_s, m_s = [], [], [], [], [], []
    cv_p, cv_s = [], []
    for i in range(depth):
        j = i // 2
        g_mix = g_mix_norm[i].reshape(1, D)
        if i % 2 == 0:
            lam_init = 0.8 - 0.6 * math.exp(-0.3 * i)
            w_qkv = attn_w_qkv[j].astype(BF16)
            w_o = attn_w_o[j].astype(BF16)
            g_sub = attn_subln[j].reshape(1, HEAD_W)
            lam_p = attn_lambda[j]
            qp, kp, vp = qkv_proj(xp, g_mix, w_qkv, rope_p, TM, L // TM)
            qs, ks, vs = qkv_proj(xs, g_mix, w_qkv, rope_s, TM_S, 1)
            ap = flash_diff_attention(qp, kp, vp, lam_p, g_sub, B, L, lam_init, TQ)
            rows = DL * N_HEADS
            as_ = paged_diff_attention(
                qs.reshape(DB, rows, HEAD_W), ks.reshape(DB, rows, HEAD_W), vs.reshape(DB, rows, HEAD_W),
                ck, cv, page_table, j * n_pool, lam_p, g_sub, lam_init, N_PG)
            xp = out_proj(xp, ap, w_o, TM)
            xs = out_proj(xs, as_.reshape(TS, D), w_o, TM_S)
            k_p.append(kp.reshape(B, L, N_HEADS, HEAD_W))
            v_p.append(vp.reshape(B, L, N_HEADS, HEAD_W))
            k_s.append(ks.reshape(DB, DL, N_HEADS, HEAD_W))
            v_s.append(vs.reshape(DB, DL, N_HEADS, HEAD_W))
        else:
            w_in = ml_w_in[j]
            w_main = w_in[:, :2 * qk_w + 2 * D].astype(BF16)
            w_gate = w_in[:, 2 * qk_w + 2 * D:]
            pad = jnp.zeros((D, LANES - N_HEADS), F32)
            wg = jnp.concatenate([w_gate[:, :N_HEADS], pad, w_gate[:, N_HEADS:], pad], axis=1).astype(BF16)
            bg = ml_b_gate[j]
            padb = jnp.zeros((LANES - N_HEADS,), F32)
            bgate = jnp.concatenate([bg[:N_HEADS], padb, bg[N_HEADS:], padb]).reshape(1, 2 * LANES)
            gnorm = ml_norm[j].reshape(1, D)
            w_out = ml_w_out[j].astype(BF16)
            qkp, vvp, oop, gtp = mlstm_in_proj(xp, g_mix, w_main, wg, TM)
            qks, vvs, oos, gts = mlstm_in_proj(xs, g_mix, w_main, wg, TM_S)
            hp, cp_, np_, mp_ = mlstm_prompt(qkp, vvp, oop, gtp, bgate, gnorm, B, L)
            n0_ext = jnp.pad(state_mlstm_n[j][..., None], ((0, 0), (0, 0), (0, 0), (0, LANES - 1)))
            m0_rows = jnp.repeat(jnp.pad(state_mlstm_m[j], ((0, 0), (0, LANES - N_HEADS))), DL, axis=0)
            hs, cs_, ns_, ms_ = mlstm_sample(qks, vvs, oos, gts, bgate, gnorm,
                                             state_mlstm_c[j], n0_ext, m0_rows, DL)
            xp = out_proj(xp, hp, w_out, TM)
            xs = out_proj(xs, hs, w_out, TM_S)
            c_p.append(cp_)
            n_p.append(np_[..., 0])
            m_p.append(mp_.reshape(B, LANES, LANES)[:, 0, :N_HEADS])
            c_s.append(cs_)
            n_s.append(ns_[..., 0])
            m_s.append(ms_.reshape(DB, DL, LANES)[:, DL - 1, :N_HEADS])
        g_ffn = g_ffn_norm[i].reshape(1, D)
        w_up = ffn_w_up[i].astype(BF16)
        w_down = ffn_w_down[i].astype(BF16)
        cw = jnp.concatenate([ffn_conv_w[i], ffn_conv_b[i][None, :],
                              jnp.zeros((SUBLANES - CONV_W - 1, d_ff), F32)], axis=0)
        fg = g_final.reshape(1, D) if i == depth - 1 else None
        xp, stp = ffn_prompt(xp, g_ffn, w_up, w_down, cw, B, L, TM_FFN, TF, fg)
        e_rows = jnp.pad(state_conv[i], ((0, 0), (0, DL - (CONV_W - 1)), (0, 0))).reshape(TS, d_ff)
        xs, gs = ffn_sample(xs, g_ffn, w_up, w_down, cw, e_rows, DL, TM_FFN_S, TF, fg)
        cv_p.append(stp.reshape(B, L // TM_FFN, SUBLANES, d_ff)[:, -1, SUBLANES - (CONV_W - 1):, :])
        cv_s.append(gs.reshape(DB, DL, d_ff)[:, DL - (CONV_W - 1):, :])

    return (xp.reshape(B, L, D), xs.reshape(DB, DL, D),
            jnp.stack(k_p), jnp.stack(v_p), jnp.stack(k_s), jnp.stack(v_s),
            jnp.stack(c_p), jnp.stack(n_p), jnp.stack(m_p),
            jnp.stack(c_s), jnp.stack(n_s), jnp.stack(m_s),
            jnp.stack(cv_p), jnp.stack(cv_s))
```

```python
import functools
import math

import jax
import jax.numpy as jnp
from jax import lax
from jax.experimental import pallas as pl
from jax.experimental.pallas import tpu as pltpu

F32 = jnp.float32
BF16 = jnp.bfloat16

LANES = 128
SUBLANES = 8
VMEM_LIMIT = 56 * 1024 * 1024

EPS = 1e-6
ROPE_THETA = 10000.0
PAGE_SIZE = 128
N_HEADS = 8
HEAD_W = 128
DA_HEAD_DIM = 64
ML_QK_DIM = 64
GATE_CAP = 15.0
CONV_W = 3
NEG = -1e30


def _cparams(sem):
    return pltpu.CompilerParams(dimension_semantics=sem, vmem_limit_bytes=VMEM_LIMIT)


def _rmsnorm_rows(x, g):
    return x * lax.rsqrt(jnp.mean(x * x, axis=-1, keepdims=True) + EPS) * g


def _rope_table_kernel(tp_ref, ts_ref, *, past, dec_seq):
    def tables(pos, shape):
        lane = lax.broadcasted_iota(jnp.int32, shape, 1)
        half = DA_HEAD_DIM // 2
        fidx = (lane & (half - 1)).astype(F32)
        freq = jnp.exp(fidx * (-math.log(ROPE_THETA) / half))
        ang = pos * freq
        c = jnp.cos(ang)
        s = jnp.sin(ang)
        first = (lane & (DA_HEAD_DIM - 1)) < half
        return c, jnp.where(first, -s, 0.0), jnp.where(first, 0.0, s)

    shp = tp_ref.shape[1:]
    pos_p = lax.broadcasted_iota(jnp.int32, shp, 0).astype(F32)
    c, s1, s2 = tables(pos_p, shp)
    tp_ref[0] = c
    tp_ref[1] = s1
    tp_ref[2] = s2
    shs = ts_ref.shape[1:]
    row = lax.broadcasted_iota(jnp.int32, shs, 0)
    pos_s = (past + (row & (dec_seq - 1))).astype(F32)
    c, s1, s2 = tables(pos_s, shs)
    ts_ref[0] = c
    ts_ref[1] = s1
    ts_ref[2] = s2


def rope_tables(seq, past, dec_seq, sample_rows):
    return pl.pallas_call(
        functools.partial(_rope_table_kernel, past=past, dec_seq=dec_seq),
        out_shape=(jax.ShapeDtypeStruct((3, seq, LANES), F32),
                   jax.ShapeDtypeStruct((3, sample_rows, LANES), F32)),
        name="rope_tables",
    )()


def _norm_into(hn_ref, x_ref, g_ref):
    hn_ref[...] = _rmsnorm_rows(x_ref[...], g_ref[...]).astype(BF16)


def _rope_head(y, c, s1, s2):
    return y * c + pltpu.roll(y, LANES - 32, 1) * s1 + pltpu.roll(y, 32, 1) * s2


def _qkv_kernel(x_ref, g_ref, w_ref, rope_ref, q_ref, k_ref, v_ref, hn_ref):
    j = pl.program_id(1)

    @pl.when(j == 0)
    def _():
        _norm_into(hn_ref, x_ref, g_ref)

    y = jnp.dot(hn_ref[...], w_ref[...], preferred_element_type=F32)

    @pl.when(j == 0)
    def _():
        c, s1, s2 = rope_ref[0], rope_ref[1], rope_ref[2]
        for h in range(N_HEADS):
            sl = slice(h * HEAD_W, (h + 1) * HEAD_W)
            q_ref[:, sl] = (_rope_head(y[:, sl], c, s1, s2) * (DA_HEAD_DIM ** -0.5)).astype(BF16)

    @pl.when(j == 1)
    def _():
        c, s1, s2 = rope_ref[0], rope_ref[1], rope_ref[2]
        for h in range(N_HEADS):
            sl = slice(h * HEAD_W, (h + 1) * HEAD_W)
            k_ref[:, sl] = _rope_head(y[:, sl], c, s1, s2)

    @pl.when(j == 2)
    def _():
        v_ref[...] = y


def qkv_proj(x, g, w_bf, rope_tab, tm, rope_blocks):
    T, D = x.shape
    n_i = T // tm

    def rope_idx(i, j):
        return (0, i % rope_blocks, 0)

    return pl.pallas_call(
        _qkv_kernel,
        grid=(n_i, 3),
        in_specs=[
            pl.BlockSpec((tm, D), lambda i, j: (i, 0)),
            pl.BlockSpec((1, D), lambda i, j: (0, 0)),
            pl.BlockSpec((D, D), lambda i, j: (0, j)),
            pl.BlockSpec((3, tm, LANES), rope_idx),
        ],
        out_specs=[
            pl.BlockSpec((tm, D), lambda i, j: (i, 0)),
            pl.BlockSpec((tm, D), lambda i, j: (i, 0)),
            pl.BlockSpec((tm, D), lambda i, j: (i, 0)),
        ],
        out_shape=(jax.ShapeDtypeStruct((T, D), BF16),
                   jax.ShapeDtypeStruct((T, D), F32),
                   jax.ShapeDtypeStruct((T, D), F32)),
        scratch_shapes=[pltpu.VMEM((tm, D), BF16)],
        compiler_params=_cparams(("arbitrary", "arbitrary")),
        name="qkv_proj",
    )(x, g, w_bf, rope_tab)


def _mlin_kernel(x_ref, g_ref, w_ref, wg_ref, qk_ref, v_ref, o_ref, gt_ref, hn_ref):
    j = pl.program_id(1)

    @pl.when(j == 0)
    def _():
        _norm_into(hn_ref, x_ref, g_ref)
        gt_ref[...] = jnp.dot(hn_ref[...], wg_ref[...], preferred_element_type=F32)

    y = jnp.dot(hn_ref[...], w_ref[...], preferred_element_type=F32)

    @pl.when(j == 0)
    def _():
        half = y.shape[1] // 2
        qk_ref[:, :half] = (y[:, :half] * (ML_QK_DIM ** -0.5)).astype(BF16)
        qk_ref[:, half:] = y[:, half:].astype(BF16)

    @pl.when(j == 1)
    def _():
        v_ref[...] = y.astype(BF16)

    @pl.when(j == 2)
    def _():
        o_ref[...] = y


def mlstm_in_proj(x, g, w_bf, wg_bf, tm):
    T, D = x.shape
    GW = wg_bf.shape[1]
    return pl.pallas_call(
        _mlin_kernel,
        grid=(T // tm, 3),
        in_specs=[
            pl.BlockSpec((tm, D), lambda i, j: (i, 0)),
            pl.BlockSpec((1, D), lambda i, j: (0, 0)),
            pl.BlockSpec((D, D), lambda i, j: (0, j)),
            pl.BlockSpec((D, GW), lambda i, j: (0, 0)),
        ],
        out_specs=[
            pl.BlockSpec((tm, D), lambda i, j: (i, 0)),
            pl.BlockSpec((tm, D), lambda i, j: (i, 0)),
            pl.BlockSpec((tm, D), lambda i, j: (i, 0)),
            pl.BlockSpec((tm, GW), lambda i, j: (i, 0)),
        ],
        out_shape=(jax.ShapeDtypeStruct((T, D), BF16),
                   jax.ShapeDtypeStruct((T, D), BF16),
                   jax.ShapeDtypeStruct((T, D), F32),
                   jax.ShapeDtypeStruct((T, GW), F32)),
        scratch_shapes=[pltpu.VMEM((tm, D), BF16)],
        compiler_params=_cparams(("arbitrary", "arbitrary")),
        name="mlstm_in_proj",
    )(x, g, w_bf, wg_bf)


def _out_proj_kernel(x_ref, a_ref, w_ref, o_ref):
    o_ref[...] = x_ref[...] + jnp.dot(a_ref[...], w_ref[...], preferred_element_type=F32)


def out_proj(x, a_bf, w_bf, tm):
    T, D = x.shape
    return pl.pallas_call(
        _out_proj_kernel,
        grid=(T // tm,),
        in_specs=[
            pl.BlockSpec((tm, D), lambda i: (i, 0)),
            pl.BlockSpec((tm, D), lambda i: (i, 0)),
            pl.BlockSpec((D, D), lambda i: (0, 0)),
        ],
        out_specs=pl.BlockSpec((tm, D), lambda i: (i, 0)),
        out_shape=jax.ShapeDtypeStruct((T, D), F32),
        compiler_params=_cparams(("arbitrary",)),
        name="out_proj",
    )(x, a_bf, w_bf)


def _diff_lambda(lam_ref, lam_init):
    lp = lam_ref[...]
    a = jnp.sum(lp[0:1] * lp[1:2], axis=-1, keepdims=True)
    b = jnp.sum(lp[2:3] * lp[3:4], axis=-1, keepdims=True)
    return jnp.exp(a) - jnp.exp(b) + lam_init


def _split_maps(q):
    qf = q.astype(F32)
    lane = lax.broadcasted_iota(jnp.int32, qf.shape, 1)
    return (jnp.where(lane < DA_HEAD_DIM, qf, 0.0).astype(BF16),
            jnp.where(lane >= DA_HEAD_DIM, qf, 0.0).astype(BF16))


def _softmax_update(s, v_bf, m_ref, l_ref, acc_ref, idx):
    m_old = m_ref[idx]
    m_new = jnp.maximum(m_old, jnp.max(s, axis=-1, keepdims=True))
    alpha = jnp.exp(m_old - m_new)
    p = jnp.exp(s - m_new)
    l_ref[idx] = alpha * l_ref[idx] + jnp.sum(p, axis=-1, keepdims=True)
    acc_ref[idx] = alpha * acc_ref[idx] + jnp.dot(p.astype(BF16), v_bf, preferred_element_type=F32)
    m_ref[idx] = m_new


def _diff_finish(o1, o2, lam, g, lam_init):
    o = o1 - lam * o2
    return (_rmsnorm_rows(o, g) * (1.0 - lam_init)).astype(BF16)


_NT = (((1,), (1,)), ((), ()))


def _flash_kernel(lam_ref, g_ref, q_ref, k_ref, v_ref, o_ref, kb, vt, m_s, l_s, acc_s, *, tq, lam_init):
    qi = pl.program_id(2)
    n_blk = vt.shape[0]

    @pl.when(qi == 0)
    def _():
        kb[...] = k_ref[...].astype(BF16)
        for c in range(n_blk):
            for cc in range(tq // LANES):
                r0 = c * tq + cc * LANES
                vt[c, :, cc * LANES:(cc + 1) * LANES] = v_ref[r0:r0 + LANES, :].T.astype(BF16)

    qm = _split_maps(q_ref[...])
    m_s[...] = jnp.full(m_s.shape, NEG, F32)
    l_s[...] = jnp.zeros(l_s.shape, F32)
    acc_s[...] = jnp.zeros(acc_s.shape, F32)

    def step(ki, masked):
        off = pl.multiple_of(ki * tq, tq)
        kblk = kb[pl.ds(off, tq), :]
        vblk = vt[ki]
        scores = [lax.dot_general(kblk, qm[mp], _NT, preferred_element_type=F32) for mp in range(2)]
        for mp in range(2):
            s = scores[mp]
            if masked:
                key = lax.broadcasted_iota(jnp.int32, s.shape, 0)
                qry = lax.broadcasted_iota(jnp.int32, s.shape, 1)
                s = jnp.where(key <= qry, s, NEG)
            m_old = m_s[mp]
            m_new = jnp.maximum(m_old, jnp.max(s, axis=0, keepdims=True))
            alpha = jnp.exp(m_old - m_new)
            p = jnp.exp(s - m_new)
            l_s[mp] = alpha * l_s[mp] + jnp.sum(p, axis=0, keepdims=True)
            acc_s[mp] = alpha * acc_s[mp] + jnp.dot(vblk, p.astype(BF16), preferred_element_type=F32)
            m_s[mp] = m_new

    def body(ki, carry):
        step(ki, False)
        return carry

    lax.fori_loop(0, qi, body, 0)
    step(qi, True)

    lam = _diff_lambda(lam_ref, lam_init)
    o_t = acc_s[0] / l_s[0] - lam * (acc_s[1] / l_s[1])
    o_ref[...] = (_rmsnorm_rows(o_t.T, g_ref[...]) * (1.0 - lam_init)).astype(BF16)


def flash_diff_attention(q_bf, k_f32, v_f32, lam_p, g_subln, batch, seq, lam_init, tq):
    T, D = q_bf.shape
    nq = seq // tq
    kern = functools.partial(_flash_kernel, tq=tq, lam_init=lam_init)
    return pl.pallas_call(
        kern,
        grid=(batch, N_HEADS, nq),
        in_specs=[
            pl.BlockSpec(lam_p.shape, lambda b, h, i: (0, 0)),
            pl.BlockSpec((1, HEAD_W), lambda b, h, i: (0, 0)),
            pl.BlockSpec((tq, HEAD_W), lambda b, h, i: (b * nq + i, h)),
            pl.BlockSpec((seq, HEAD_W), lambda b, h, i: (b, h)),
            pl.BlockSpec((seq, HEAD_W), lambda b, h, i: (b, h)),
        ],
        out_specs=pl.BlockSpec((tq, HEAD_W), lambda b, h, i: (b * nq + i, h)),
        out_shape=jax.ShapeDtypeStruct((T, D), BF16),
        scratch_shapes=[
            pltpu.VMEM((seq, HEAD_W), BF16),
            pltpu.VMEM((nq, HEAD_W, tq), BF16),
            pltpu.VMEM((2, 1, tq), F32),
            pltpu.VMEM((2, 1, tq), F32),
            pltpu.VMEM((2, HEAD_W, tq), F32),
        ],
        compiler_params=_cparams(("arbitrary", "arbitrary", "arbitrary")),
        name="flash_diff_attention",
    )(lam_p, g_subln, q_bf, k_f32, v_f32)


def _paged_kernel(pt_ref, lam_ref, g_ref, q_ref, kn_ref, vn_ref, *rest, n_pg, lam_init):
    del pt_ref
    k_refs = rest[:n_pg]
    v_refs = rest[n_pg:2 * n_pg]
    o_ref = rest[2 * n_pg]
    qbd, bias, m_s, l_s, acc_s = rest[2 * n_pg + 1:]
    pg = pl.program_id(1)
    rows = q_ref.shape[1]

    @pl.when(pg == 0)
    def _():
        q1, q2 = _split_maps(q_ref[0])
        qbd[0:rows] = q1
        qbd[rows:2 * rows] = q2
        r = lax.broadcasted_iota(jnp.int32, bias.shape, 0)
        c = lax.broadcasted_iota(jnp.int32, bias.shape, 1)
        bias[...] = jnp.where((r & (N_HEADS - 1)) == (c & (N_HEADS - 1)), 0.0, NEG)
        m_s[...] = jnp.full(m_s.shape, NEG, F32)
        l_s[...] = jnp.zeros(l_s.shape, F32)
        acc_s[...] = jnp.zeros(acc_s.shape, F32)

    q_all = qbd[...]
    ss = [lax.dot_general(q_all, k_refs[r][0].astype(BF16), _NT, preferred_element_type=F32) + bias[...]
          for r in range(n_pg)]
    m_old = m_s[0]
    m_new = m_old
    for s in ss:
        m_new = jnp.maximum(m_new, jnp.max(s, axis=-1, keepdims=True))
    alpha = jnp.exp(m_old - m_new)
    l_new = alpha * l_s[0]
    acc_new = alpha * acc_s[0]
    for r in range(n_pg):
        p = jnp.exp(ss[r] - m_new)
        l_new = l_new + jnp.sum(p, axis=-1, keepdims=True)
        acc_new = acc_new + jnp.dot(p.astype(BF16), v_refs[r][0].astype(BF16), preferred_element_type=F32)
    m_s[0] = m_new
    l_s[0] = l_new
    acc_s[0] = acc_new

    @pl.when(pg == pl.num_programs(1) - 1)
    def _():
        kn = kn_ref[0].astype(BF16)
        vn = vn_ref[0].astype(BF16)
        s = lax.dot_general(qbd[...], kn, _NT, preferred_element_type=F32)
        rr = lax.broadcasted_iota(jnp.int32, s.shape, 0)
        cc = lax.broadcasted_iota(jnp.int32, s.shape, 1)
        same_head = (rr & (N_HEADS - 1)) == (cc & (N_HEADS - 1))
        q_tok = (rr & (rows - 1)) >> 3
        k_tok = cc >> 3
        s = jnp.where(same_head, jnp.where(k_tok <= q_tok, s, NEG), NEG)
        _softmax_update(s, vn, m_s, l_s, acc_s, 0)
        o = acc_s[0] / l_s[0]
        lam = _diff_lambda(lam_ref, lam_init)
        o_ref[0] = _diff_finish(o[0:rows], o[rows:2 * rows], lam, g_ref[...], lam_init)


def paged_diff_attention(q_bf, k_new, v_new, cache_k, cache_v, page_table, pool_offset, lam_p, g_subln,
                         lam_init, n_pg):
    DB, rows, _ = q_bf.shape
    n_pages = page_table.shape[1]
    page_rows = cache_k.shape[1]
    steps = n_pages // n_pg

    def page_idx(r):
        def idx(b, p, pt):
            return (pool_offset + pt[b * n_pages + p * n_pg + r], 0, 0)
        return idx

    seq_idx = lambda b, p, pt: (b, 0, 0)
    const2 = lambda b, p, pt: (0, 0)
    in_specs = [
        pl.BlockSpec(lam_p.shape, const2),
        pl.BlockSpec((1, HEAD_W), const2),
        pl.BlockSpec((1, rows, HEAD_W), seq_idx),
        pl.BlockSpec((1, rows, HEAD_W), seq_idx),
        pl.BlockSpec((1, rows, HEAD_W), seq_idx),
    ]
    in_specs += [pl.BlockSpec((1, page_rows, HEAD_W), page_idx(r)) for r in range(n_pg)]
    in_specs += [pl.BlockSpec((1, page_rows, HEAD_W), page_idx(r)) for r in range(n_pg)]
    grid_spec = pltpu.PrefetchScalarGridSpec(
        num_scalar_prefetch=1,
        grid=(DB, steps),
        in_specs=in_specs,
        out_specs=pl.BlockSpec((1, rows, HEAD_W), seq_idx),
        scratch_shapes=[
            pltpu.VMEM((2 * rows, HEAD_W), BF16),
            pltpu.VMEM((2 * rows, page_rows), F32),
            pltpu.VMEM((1, 2 * rows, 1), F32),
            pltpu.VMEM((1, 2 * rows, 1), F32),
            pltpu.VMEM((1, 2 * rows, HEAD_W), F32),
        ],
    )
    kern = functools.partial(_paged_kernel, n_pg=n_pg, lam_init=lam_init)
    return pl.pallas_call(
        kern,
        grid_spec=grid_spec,
        out_shape=jax.ShapeDtypeStruct((DB, rows, HEAD_W), BF16),
        compiler_params=_cparams(("arbitrary", "arbitrary")),
        name="paged_diff_attention",
    )(page_table.reshape(-1), lam_p, g_subln, q_bf, k_new, v_new,
      *([cache_k] * n_pg), *([cache_v] * n_pg))


def _seg_scan(x, pos, seg, op, ident):
    s = 1
    while s < seg:
        shifted = pltpu.roll(x, s, 0)
        x = op(x, jnp.where(pos >= s, shifted, ident))
        s *= 2
    return x


def _seg_last(x, n_seg, seg):
    r, w = x.shape
    x3 = x.reshape(n_seg, seg, w)
    return jnp.broadcast_to(x3[:, seg - 1:seg, :], (n_seg, seg, w)).reshape(r, w)


def _mlstm_kernel(*refs, n_seg, seg, has_state):
    if has_state:
        (qk_ref, v_ref, o_ref, gt_ref, bg_ref, gn_ref, c_in, n_in, m_in,
         h_ref, c_out, n_out, m_out) = refs
    else:
        qk_ref, v_ref, o_ref, gt_ref, bg_ref, gn_ref, h_ref, c_out, n_out, m_out = refs
        c_in, n_in, m_in = c_out, n_out, m_out

        @pl.when(pl.program_id(1) == 0)
        def _():
            c_out[...] = jnp.zeros(c_out.shape, F32)
            n_out[...] = jnp.zeros(n_out.shape, F32)
            m_out[...] = jnp.zeros(m_out.shape, F32)

    R = n_seg * seg
    D = qk_ref.shape[1]
    KOFF = D // 2
    QK = ML_QK_DIM

    pre = gt_ref[...] + bg_ref[...]
    pre = GATE_CAP * jnp.tanh(pre * (1.0 / GATE_CAP))
    li = pre[:, :LANES]
    xf = pre[:, LANES:]
    lf = jnp.minimum(xf, 0.0) - jnp.log1p(jnp.exp(-jnp.abs(xf)))

    rowi = lax.broadcasted_iota(jnp.int32, (R, LANES), 0)
    pos = rowi & (seg - 1)
    bcum = _seg_scan(lf, pos, seg, jnp.add, 0.0)
    a = li - bcum
    cm = _seg_scan(a, pos, seg, jnp.maximum, NEG)
    m0 = m_in[...]
    mx = jnp.maximum(m0, cm)
    mt = bcum + mx
    u = -mx
    w_inter = jnp.exp(m0 + u)
    emt = jnp.exp(-mt)
    b_last = _seg_last(bcum, n_seg, seg)
    m_new = _seg_last(mt, n_seg, seg)
    decay = jnp.exp(b_last + m0 - m_new)
    w_state = jnp.exp(a + b_last - m_new)
    a_t = a.T

    tr = lax.broadcasted_iota(jnp.int32, (R, R), 0)
    tc = lax.broadcasted_iota(jnp.int32, (R, R), 1)
    if n_seg == 1:
        intra = tc <= tr
    else:
        shift = seg.bit_length() - 1
        intra = ((tr >> shift) == (tc >> shift)) & (tc <= tr)

    lane = lax.broadcasted_iota(jnp.int32, (R, LANES), 1)
    low = lane < QK
    hr = lax.broadcasted_iota(jnp.int32, (LANES, LANES), 0)
    hc = lax.broadcasted_iota(jnp.int32, (LANES, LANES), 1)
    half_sel = jnp.where(hc == jnp.where(hr < QK, 0, 1), 1.0, 0.0).astype(BF16)
    low_row = low[0:1]
    if n_seg > 1:
        colseg = lax.broadcasted_iota(jnp.int32, (QK, R), 1) >> (seg.bit_length() - 1)

    for p in range(N_HEADS // 2):
        ha, hb = 2 * p, 2 * p + 1
        qp = qk_ref[:, p * LANES:(p + 1) * LANES].astype(F32)
        kp_bf = qk_ref[:, KOFF + p * LANES:KOFF + (p + 1) * LANES]
        q_heads = (jnp.where(low, qp, 0.0).astype(BF16), jnp.where(low, 0.0, qp).astype(BF16))
        kw = kp_bf.astype(F32) * jnp.where(low, w_state[:, ha:ha + 1], w_state[:, hb:hb + 1])
        kw_t = kw.T
        if n_seg == 1:
            n_rows = n_in[0, p:p + 1, :]
        else:
            n_rows = jnp.concatenate(
                [jnp.broadcast_to(n_in[g, p:p + 1, :], (seg, LANES)) for g in range(n_seg)], axis=0)
        qn = jnp.dot((qp * n_rows).astype(BF16), half_sel, preferred_element_type=F32)
        sp = [c_in[g, ha:hb + 1].reshape(2 * QK, LANES).astype(BF16) for g in range(n_seg)]
        for hi, h in enumerate((ha, hb)):
            qh = q_heads[hi]
            s = lax.dot_general(qh, kp_bf, _NT, preferred_element_type=F32)
            wi = jnp.where(intra, jnp.exp(u[:, h:h + 1] + a_t[h:h + 1, :]), 0.0)
            sw = wi * s
            vh = v_ref[:, h * LANES:(h + 1) * LANES]
            num = jnp.dot(sw.astype(BF16), vh, preferred_element_type=F32)
            den = jnp.sum(sw, axis=-1, keepdims=True)
            if n_seg == 1:
                inter = jnp.dot(qh, sp[0], preferred_element_type=F32)
            else:
                inter = jnp.concatenate(
                    [jnp.dot(qh[g * seg:(g + 1) * seg], sp[g], preferred_element_type=F32)
                     for g in range(n_seg)], axis=0)
            wcol = w_inter[:, h:h + 1]
            num = num + inter * wcol
            den = den + wcol * qn[:, hi:hi + 1]
            denom = jnp.maximum(jnp.abs(den), emt[:, h:h + 1])
            hh = num / denom
            hn = _rmsnorm_rows(hh, gn_ref[:, h * LANES:(h + 1) * LANES])
            og = jax.nn.sigmoid(o_ref[:, h * LANES:(h + 1) * LANES])
            h_ref[:, h * LANES:(h + 1) * LANES] = (og * hn).astype(BF16)
            kw_h = kw_t[hi * QK:(hi + 1) * QK, :]
            if n_seg == 1:
                upd = jnp.dot(kw_h.astype(BF16), vh, preferred_element_type=F32)
                c_out[0, h] = decay[0:1, h:h + 1] * c_in[0, h] + upd
            else:
                lhs = jnp.concatenate(
                    [jnp.where(colseg == g, kw_h, 0.0) for g in range(n_seg)], axis=0).astype(BF16)
                upd = jnp.dot(lhs, vh, preferred_element_type=F32)
                for g in range(n_seg):
                    c_out[g, h] = decay[g * seg:g * seg + 1, h:h + 1] * c_in[g, h] + upd[g * QK:(g + 1) * QK]
        for g in range(n_seg):
            r0 = g * seg
            ksum = jnp.sum(kw[r0:r0 + seg], axis=0, keepdims=True)
            dpair = jnp.where(low_row, decay[r0:r0 + 1, ha:ha + 1], decay[r0:r0 + 1, hb:hb + 1])
            n_out[g, p:p + 1, :] = dpair * n_in[g, p:p + 1, :] + ksum
    m_out[...] = m_new


def mlstm_prompt(qk, v, o, gates, bgate, gnorm, batch, seq):
    T, D = qk.shape
    R = LANES
    nc = seq // R
    GW = gates.shape[1]
    row = lambda b, c: (b * nc + c, 0)
    const = lambda b, c: (0, 0)
    st = lambda b, c: (b, 0, 0, 0)
    kern = functools.partial(_mlstm_kernel, n_seg=1, seg=R, has_state=False)
    return pl.pallas_call(
        kern,
        grid=(batch, nc),
        in_specs=[
            pl.BlockSpec((R, D), row), pl.BlockSpec((R, D), row), pl.BlockSpec((R, D), row),
            pl.BlockSpec((R, GW), row), pl.BlockSpec((1, GW), const), pl.BlockSpec((1, D), const),
        ],
        out_specs=[
            pl.BlockSpec((R, D), row),
            pl.BlockSpec((1, N_HEADS, ML_QK_DIM, LANES), st),
            pl.BlockSpec((1, N_HEADS // 2, LANES), lambda b, c: (b, 0, 0)),
            pl.BlockSpec((R, LANES), lambda b, c: (b, 0)),
        ],
        out_shape=(jax.ShapeDtypeStruct((T, D), BF16),
                   jax.ShapeDtypeStruct((batch, N_HEADS, ML_QK_DIM, LANES), F32),
                   jax.ShapeDtypeStruct((batch, N_HEADS // 2, LANES), F32),
                   jax.ShapeDtypeStruct((batch * R, LANES), F32)),
        compiler_params=_cparams(("arbitrary", "arbitrary")),
        name="mlstm_prompt",
    )(qk, v, o, gates, bgate, gnorm)


def mlstm_sample(qk, v, o, gates, bgate, gnorm, c0, n0_pairs, m0_rows, seg):
    T, D = qk.shape
    R = LANES
    n_seg = R // seg
    GW = gates.shape[1]
    row = lambda i: (i, 0)
    const = lambda i: (0, 0)
    st = lambda i: (i, 0, 0, 0)
    st_block = (n_seg, N_HEADS, ML_QK_DIM, LANES)
    n_block = (n_seg, N_HEADS // 2, LANES)
    nst = lambda i: (i, 0, 0)
    kern = functools.partial(_mlstm_kernel, n_seg=n_seg, seg=seg, has_state=True)
    return pl.pallas_call(
        kern,
        grid=(T // R,),
        in_specs=[
            pl.BlockSpec((R, D), row), pl.BlockSpec((R, D), row), pl.BlockSpec((R, D), row),
            pl.BlockSpec((R, GW), row), pl.BlockSpec((1, GW), const), pl.BlockSpec((1, D), const),
            pl.BlockSpec(st_block, st), pl.BlockSpec(n_block, nst), pl.BlockSpec((R, LANES), row),
        ],
        out_specs=[
            pl.BlockSpec((R, D), row),
            pl.BlockSpec(st_block, st),
            pl.BlockSpec(n_block, nst),
            pl.BlockSpec((R, LANES), row),
        ],
        out_shape=(jax.ShapeDtypeStruct((T, D), BF16),
                   jax.ShapeDtypeStruct(c0.shape, F32),
                   jax.ShapeDtypeStruct(n0_pairs.shape, F32),
                   jax.ShapeDtypeStruct((T, LANES), F32)),
        compiler_params=_cparams(("arbitrary",)),
        name="mlstm_sample",
    )(qk, v, o, gates, bgate, gnorm, c0, n0_pairs, m0_rows)


def _ffn_core(x_ref, gn_ref, wg_ref, wu_ref, wd_ref, cw_ref, y_ref, hn_ref, shifted, final_g_ref):
    f = pl.program_id(1)

    @pl.when(f == 0)
    def _():
        _norm_into(hn_ref, x_ref, gn_ref)
        y_ref[...] = x_ref[...]

    hn = hn_ref[...]
    g = jnp.dot(hn, wg_ref[...], preferred_element_type=F32)
    u = jnp.dot(hn, wu_ref[...], preferred_element_type=F32)
    g1, g2 = shifted(g)
    cw = cw_ref[...]
    gc = cw[3:4] + cw[0:1] * g2 + cw[1:2] * g1 + cw[2:3] * g
    act = (jax.nn.silu(gc) * u).astype(BF16)
    y_ref[...] += jnp.dot(act, wd_ref[...], preferred_element_type=F32)

    if final_g_ref is not None:
        @pl.when(f == pl.num_programs(1) - 1)
        def _():
            y_ref[...] = _rmsnorm_rows(y_ref[...], final_g_ref[...])
    return g


def _ffn_prompt_kernel(*refs, tiles_per_seq, final_norm):
    if final_norm:
        x_ref, gn_ref, wg_ref, wu_ref, wd_ref, cw_ref, fg_ref, y_ref, st_ref, hn_ref, gbuf, carry = refs
    else:
        x_ref, gn_ref, wg_ref, wu_ref, wd_ref, cw_ref, y_ref, st_ref, hn_ref, gbuf, carry = refs
        fg_ref = None
    i = pl.program_id(0)
    f = pl.program_id(1)
    tm = x_ref.shape[0]

    @pl.when((i == 0) & (f == 0))
    def _():
        carry[...] = jnp.zeros(carry.shape, F32)

    def shifted(g):
        prev = jnp.where(i % tiles_per_seq == 0, 0.0, carry[f])
        gbuf[0:SUBLANES] = prev
        gbuf[SUBLANES:] = g
        return gbuf[SUBLANES - 1:SUBLANES - 1 + tm], gbuf[SUBLANES - 2:SUBLANES - 2 + tm]

    g = _ffn_core(x_ref, gn_ref, wg_ref, wu_ref, wd_ref, cw_ref, y_ref, hn_ref, shifted, fg_ref)
    tail = g[tm - SUBLANES:]
    carry[f] = tail
    st_ref[0] = tail


def _ffn_sample_kernel(*refs, seg, final_norm):
    if final_norm:
        x_ref, gn_ref, wg_ref, wu_ref, wd_ref, cw_ref, e_ref, fg_ref, y_ref, g_ref, hn_ref = refs
    else:
        x_ref, gn_ref, wg_ref, wu_ref, wd_ref, cw_ref, e_ref, y_ref, g_ref, hn_ref = refs
        fg_ref = None
    tm = x_ref.shape[0]

    def shifted(g):
        e = e_ref[...]
        pos = lax.broadcasted_iota(jnp.int32, g.shape, 0) & (seg - 1)
        g1 = jnp.where(pos < 1, pltpu.roll(e, tm - 1, 0), pltpu.roll(g, 1, 0))
        g2 = jnp.where(pos < 2, e, pltpu.roll(g, 2, 0))
        return g1, g2

    g = _ffn_core(x_ref, gn_ref, wg_ref, wu_ref, wd_ref, cw_ref, y_ref, hn_ref, shifted, fg_ref)
    g_ref[...] = g


def _ffn_common_specs(D, tm, tf, n_f):
    return [
        pl.BlockSpec((tm, D), lambda i, f: (i, 0)),
        pl.BlockSpec((1, D), lambda i, f: (0, 0)),
        pl.BlockSpec((D, tf), lambda i, f: (0, f)),
        pl.BlockSpec((D, tf), lambda i, f: (0, n_f + f)),
        pl.BlockSpec((tf, D), lambda i, f: (f, 0)),
        pl.BlockSpec((SUBLANES, tf), lambda i, f: (0, f)),
    ]


def ffn_prompt(x, gn, w_up_bf, w_down_bf, cw, batch, seq, tm, tf, final_g=None):
    T, D = x.shape
    F = w_down_bf.shape[0]
    n_f = F // tf
    tps = seq // tm
    in_specs = _ffn_common_specs(D, tm, tf, n_f)
    args = [x, gn, w_up_bf, w_up_bf, w_down_bf, cw]
    if final_g is not None:
        in_specs.append(pl.BlockSpec((1, D), lambda i, f: (0, 0)))
        args.append(final_g)
    kern = functools.partial(_ffn_prompt_kernel, tiles_per_seq=tps, final_norm=final_g is not None)
    return pl.pallas_call(
        kern,
        grid=(T // tm, n_f),
        in_specs=in_specs,
        out_specs=[
            pl.BlockSpec((tm, D), lambda i, f: (i, 0)),
            pl.BlockSpec((1, SUBLANES, tf), lambda i, f: (i, 0, f)),
        ],
        out_shape=(jax.ShapeDtypeStruct((T, D), F32),
                   jax.ShapeDtypeStruct((T // tm, SUBLANES, F), F32)),
        scratch_shapes=[
            pltpu.VMEM((tm, D), BF16),
            pltpu.VMEM((tm + SUBLANES, tf), F32),
            pltpu.VMEM((n_f, SUBLANES, tf), F32),
        ],
        compiler_params=_cparams(("arbitrary", "arbitrary")),
        name="ffn_prompt",
    )(*args)


def ffn_sample(x, gn, w_up_bf, w_down_bf, cw, e_rows, seg, tm, tf, final_g=None):
    T, D = x.shape
    F = w_down_bf.shape[0]
    n_f = F // tf
    in_specs = _ffn_common_specs(D, tm, tf, n_f)
    in_specs.append(pl.BlockSpec((tm, tf), lambda i, f: (i, f)))
    args = [x, gn, w_up_bf, w_up_bf, w_down_bf, cw, e_rows]
    if final_g is not None:
        in_specs.append(pl.BlockSpec((1, D), lambda i, f: (0, 0)))
        args.append(final_g)
    kern = functools.partial(_ffn_sample_kernel, seg=seg, final_norm=final_g is not None)
    return pl.pallas_call(
        kern,
        grid=(T // tm, n_f),
        in_specs=in_specs,
        out_specs=[
            pl.BlockSpec((tm, D), lambda i, f: (i, 0)),
            pl.BlockSpec((tm, tf), lambda i, f: (i, f)),
        ],
        out_shape=(jax.ShapeDtypeStruct((T, D), F32),
                   jax.ShapeDtypeStruct((T, F), F32)),
        scratch_shapes=[pltpu.VMEM((tm, D), BF16)],
        compiler_params=_cparams(("arbitrary", "arbitrary")),
        name="ffn_sample",
    )(*args)


def kernel(x_prompt, x_sample, cache_k, cache_v, page_table, state_mlstm_c, state_mlstm_n, state_mlstm_m,
           state_conv, g_mix_norm, g_ffn_norm, g_final, attn_w_qkv, attn_lambda, attn_subln, attn_w_o,
           ml_w_in, ml_b_gate, ml_norm, ml_w_out, ffn_w_up, ffn_conv_w, ffn_conv_b, ffn_w_down):
    B, L, D = x_prompt.shape
    DB, DL, _ = x_sample.shape
    depth = g_mix_norm.shape[0]
    n_attn, n_pool = cache_k.shape[:2]
    n_pages = page_table.shape[1]
    past = n_pages * PAGE_SIZE
    d_ff = ffn_w_down.shape[1]
    n_ml_heads = state_mlstm_c.shape[2]
    assert D == N_HEADS * HEAD_W and n_ml_heads == N_HEADS and DL == SUBLANES
    assert cache_k.shape[2:] == (PAGE_SIZE, N_HEADS, HEAD_W) and cache_v.shape[2:] == (PAGE_SIZE, N_HEADS, HEAD_W)

    TP, TS = B * L, DB * DL
    TM = 512
    TM_S = min(TM, TS)
    TM_FFN = 1024
    TM_FFN_S = min(TM_FFN, TS)
    TF = 256
    TQ = 512
    N_PG = 4

    xp = x_prompt.reshape(TP, D)
    xs = x_sample.reshape(TS, D)
    rope_p, rope_s = rope_tables(L, past, DL, TM_S)
    ck = cache_k.reshape(n_attn * n_pool, PAGE_SIZE * N_HEADS, HEAD_W)
    cv = cache_v.reshape(n_attn * n_pool, PAGE_SIZE * N_HEADS, HEAD_W)

    qk_w = N_HEADS * ML_QK_DIM
    k_p, v_p, k_s, v_s = [], [], [], []
    c_p, n_p, m_p, c_s, n_s, m_s = [], [], [], [], [], []
    cv_p, cv_s = [], []
    for i in range(depth):
        j = i // 2
        g_mix = g_mix_norm[i].reshape(1, D)
        if i % 2 == 0:
            lam_init = 0.8 - 0.6 * math.exp(-0.3 * i)
            w_qkv = attn_w_qkv[j].astype(BF16)
            w_o = attn_w_o[j].astype(BF16)
            g_sub = attn_subln[j].reshape(1, HEAD_W)
            lam_p = attn_lambda[j]
            qp, kp, vp = qkv_proj(xp, g_mix, w_qkv, rope_p, TM, L // TM)
            qs, ks, vs = qkv_proj(xs, g_mix, w_qkv, rope_s, TM_S, 1)
            ap = flash_diff_attention(qp, kp, vp, lam_p, g_sub, B, L, lam_init, TQ)
            rows = DL * N_HEADS
            as_ = paged_diff_attention(
                qs.reshape(DB, rows, HEAD_W), ks.reshape(DB, rows, HEAD_W), vs.reshape(DB, rows, HEAD_W),
                ck, cv, page_table, j * n_pool, lam_p, g_sub, lam_init, N_PG)
            xp = out_proj(xp, ap, w_o, TM)
            xs = out_proj(xs, as_.reshape(TS, D), w_o, TM_S)
            k_p.append(kp.reshape(B, L, N_HEADS, HEAD_W))
            v_p.append(vp.reshape(B, L, N_HEADS, HEAD_W))
            k_s.append(ks.reshape(DB, DL, N_HEADS, HEAD_W))
            v_s.append(vs.reshape(DB, DL, N_HEADS, HEAD_W))
        else:
            w_in = ml_w_in[j]
            w_main = w_in[:, :2 * qk_w + 2 * D].astype(BF16)
            w_gate = w_in[:, 2 * qk_w + 2 * D:]
            pad = jnp.zeros((D, LANES - N_HEADS), F32)
            wg = jnp.concatenate([w_gate[:, :N_HEADS], pad, w_gate[:, N_HEADS:], pad], axis=1).astype(BF16)
            bg = ml_b_gate[j]
            padb = jnp.zeros((LANES - N_HEADS,), F32)
            bgate = jnp.concatenate([bg[:N_HEADS], padb, bg[N_HEADS:], padb]).reshape(1, 2 * LANES)
            gnorm = ml_norm[j].reshape(1, D)
            w_out = ml_w_out[j].astype(BF16)
            qkp, vvp, oop, gtp = mlstm_in_proj(xp, g_mix, w_main, wg, TM)
            qks, vvs, oos, gts = mlstm_in_proj(xs, g_mix, w_main, wg, TM_S)
            hp, cp_, np_, mp_ = mlstm_prompt(qkp, vvp, oop, gtp, bgate, gnorm, B, L)
            n0_pairs = state_mlstm_n[j].reshape(DB, N_HEADS // 2, LANES)
            m0_rows = jnp.repeat(jnp.pad(state_mlstm_m[j], ((0, 0), (0, LANES - N_HEADS))), DL, axis=0)
            hs, cs_, ns_, ms_ = mlstm_sample(qks, vvs, oos, gts, bgate, gnorm,
                                             state_mlstm_c[j], n0_pairs, m0_rows, DL)
            xp = out_proj(xp, hp, w_out, TM)
            xs = out_proj(xs, hs, w_out, TM_S)
            c_p.append(cp_)
            n_p.append(np_.reshape(B, N_HEADS, ML_QK_DIM))
            m_p.append(mp_.reshape(B, LANES, LANES)[:, 0, :N_HEADS])
            c_s.append(cs_)
            n_s.append(ns_.reshape(DB, N_HEADS, ML_QK_DIM))
            m_s.append(ms_.reshape(DB, DL, LANES)[:, DL - 1, :N_HEADS])
        g_ffn = g_ffn_norm[i].reshape(1, D)
        w_up = ffn_w_up[i].astype(BF16)
        w_down = ffn_w_down[i].astype(BF16)
        cw = jnp.concatenate([ffn_conv_w[i], ffn_conv_b[i][None, :],
                              jnp.zeros((SUBLANES - CONV_W - 1, d_ff), F32)], axis=0)
        fg = g_final.reshape(1, D) if i == depth - 1 else None
        xp, stp = ffn_prompt(xp, g_ffn, w_up, w_down, cw, B, L, TM_FFN, TF, fg)
        e_rows = jnp.pad(state_conv[i], ((0, 0), (0, DL - (CONV_W - 1)), (0, 0))).reshape(TS, d_ff)
        xs, gs = ffn_sample(xs, g_ffn, w_up, w_down, cw, e_rows, DL, TM_FFN_S, TF, fg)
        cv_p.append(stp.reshape(B, L // TM_FFN, SUBLANES, d_ff)[:, -1, SUBLANES - (CONV_W - 1):, :])
        cv_s.append(gs.reshape(DB, DL, d_ff)[:, DL - (CONV_W - 1):, :])

    return (xp.reshape(B, L, D), xs.reshape(DB, DL, D),
            jnp.stack(k_p), jnp.stack(v_p), jnp.stack(k_s), jnp.stack(v_s),
            jnp.stack(c_p), jnp.stack(n_p), jnp.stack(m_p),
            jnp.stack(c_s), jnp.stack(n_s), jnp.stack(m_s),
            jnp.stack(cv_p), jnp.stack(cv_s))
```

```python
import functools
import math

import jax
import jax.numpy as jnp
from jax import lax
from jax.experimental import pallas as pl
from jax.experimental.pallas import tpu as pltpu

F32 = jnp.float32
BF16 = jnp.bfloat16

LANES = 128
SUBLANES = 8
VMEM_LIMIT = 56 * 1024 * 1024

EPS = 1e-6
ROPE_THETA = 10000.0
PAGE_SIZE = 128
N_HEADS = 8
HEAD_W = 128
DA_HEAD_DIM = 64
ML_QK_DIM = 64
GATE_CAP = 15.0
CONV_W = 3
NEG = -1e30


def _cparams(sem):
    return pltpu.CompilerParams(dimension_semantics=sem, vmem_limit_bytes=VMEM_LIMIT)


def _rmsnorm_rows(x, g):
    return x * lax.rsqrt(jnp.mean(x * x, axis=-1, keepdims=True) + EPS) * g


def _rope_table_kernel(tp_ref, ts_ref, *, past, dec_seq):
    def tables(pos, shape):
        lane = lax.broadcasted_iota(jnp.int32, shape, 1)
        half = DA_HEAD_DIM // 2
        fidx = (lane & (half - 1)).astype(F32)
        freq = jnp.exp(fidx * (-math.log(ROPE_THETA) / half))
        ang = pos * freq
        c = jnp.cos(ang)
        s = jnp.sin(ang)
        first = (lane & (DA_HEAD_DIM - 1)) < half
        return c, jnp.where(first, -s, 0.0), jnp.where(first, 0.0, s)

    shp = tp_ref.shape[1:]
    pos_p = lax.broadcasted_iota(jnp.int32, shp, 0).astype(F32)
    c, s1, s2 = tables(pos_p, shp)
    tp_ref[0] = c
    tp_ref[1] = s1
    tp_ref[2] = s2
    shs = ts_ref.shape[1:]
    row = lax.broadcasted_iota(jnp.int32, shs, 0)
    pos_s = (past + (row & (dec_seq - 1))).astype(F32)
    c, s1, s2 = tables(pos_s, shs)
    ts_ref[0] = c
    ts_ref[1] = s1
    ts_ref[2] = s2


def rope_tables(seq, past, dec_seq, sample_rows):
    return pl.pallas_call(
        functools.partial(_rope_table_kernel, past=past, dec_seq=dec_seq),
        out_shape=(jax.ShapeDtypeStruct((3, seq, LANES), F32),
                   jax.ShapeDtypeStruct((3, sample_rows, LANES), F32)),
        name="rope_tables",
    )()


def _norm_into(hn_ref, x_ref, g_ref):
    hn_ref[...] = _rmsnorm_rows(x_ref[...], g_ref[...]).astype(BF16)


def _rope_head(y, c, s1, s2):
    return y * c + pltpu.roll(y, LANES - 32, 1) * s1 + pltpu.roll(y, 32, 1) * s2


def _qkv_kernel(x_ref, g_ref, w_ref, rope_ref, q_ref, kb_ref, vb_ref, k3_ref, v3_ref, hn_ref, tr_ref):
    j = pl.program_id(1)
    tm = x_ref.shape[0]

    @pl.when(j == 0)
    def _():
        _norm_into(hn_ref, x_ref, g_ref)

    y = jnp.dot(hn_ref[...], w_ref[...], preferred_element_type=F32)

    @pl.when(j == 0)
    def _():
        c, s1, s2 = rope_ref[0], rope_ref[1], rope_ref[2]
        for h in range(N_HEADS):
            sl = slice(h * HEAD_W, (h + 1) * HEAD_W)
            q_ref[:, sl] = (_rope_head(y[:, sl], c, s1, s2) * (DA_HEAD_DIM ** -0.5)).astype(BF16)

    def head_tile(h):
        return slice(h * SUBLANES, (h + 1) * SUBLANES)

    def to_cache_rows(out_ref):
        for s in range(SUBLANES):
            out_ref[:, s] = tr_ref[:, pl.ds(s, N_HEADS, stride=SUBLANES), :]

    @pl.when(j == 1)
    def _():
        c, s1, s2 = rope_ref[0], rope_ref[1], rope_ref[2]
        for h in range(N_HEADS):
            sl = slice(h * HEAD_W, (h + 1) * HEAD_W)
            kh = _rope_head(y[:, sl], c, s1, s2)
            kb_ref[:, sl] = kh.astype(BF16)
            tr_ref[:, head_tile(h), :] = kh.reshape(tm // SUBLANES, SUBLANES, HEAD_W)
        to_cache_rows(k3_ref)

    @pl.when(j == 2)
    def _():
        vb_ref[...] = y.astype(BF16)
        for h in range(N_HEADS):
            tr_ref[:, head_tile(h), :] = y[:, h * HEAD_W:(h + 1) * HEAD_W].reshape(
                tm // SUBLANES, SUBLANES, HEAD_W)
        to_cache_rows(v3_ref)


def qkv_proj(x, g, w_bf, rope_tab, tm, rope_blocks):
    T, D = x.shape
    n_i = T // tm

    def rope_idx(i, j):
        return (0, i % rope_blocks, 0)

    flat = pl.BlockSpec((tm, D), lambda i, j: (i, 0))
    tg = tm // SUBLANES
    rows3 = pl.BlockSpec((tg, SUBLANES, N_HEADS, HEAD_W), lambda i, j: (i, 0, 0, 0))
    return pl.pallas_call(
        _qkv_kernel,
        grid=(n_i, 3),
        in_specs=[
            flat,
            pl.BlockSpec((1, D), lambda i, j: (0, 0)),
            pl.BlockSpec((D, D), lambda i, j: (0, j)),
            pl.BlockSpec((3, tm, LANES), rope_idx),
        ],
        out_specs=[flat, flat, flat, rows3, rows3],
        out_shape=(jax.ShapeDtypeStruct((T, D), BF16),
                   jax.ShapeDtypeStruct((T, D), BF16),
                   jax.ShapeDtypeStruct((T, D), BF16),
                   jax.ShapeDtypeStruct((T // SUBLANES, SUBLANES, N_HEADS, HEAD_W), F32),
                   jax.ShapeDtypeStruct((T // SUBLANES, SUBLANES, N_HEADS, HEAD_W), F32)),
        scratch_shapes=[pltpu.VMEM((tm, D), BF16),
                        pltpu.VMEM((tg, N_HEADS * SUBLANES, HEAD_W), F32)],
        compiler_params=_cparams(("arbitrary", "arbitrary")),
        name="qkv_proj",
    )(x, g, w_bf, rope_tab)


def _mlin_kernel(x_ref, g_ref, w_ref, wg_ref, qk_ref, v_ref, o_ref, gt_ref, hn_ref):
    j = pl.program_id(1)

    @pl.when(j == 0)
    def _():
        _norm_into(hn_ref, x_ref, g_ref)
        gt_ref[...] = jnp.dot(hn_ref[...], wg_ref[...], preferred_element_type=F32)

    y = jnp.dot(hn_ref[...], w_ref[...], preferred_element_type=F32)

    @pl.when(j == 0)
    def _():
        half = y.shape[1] // 2
        qk_ref[:, :half] = (y[:, :half] * (ML_QK_DIM ** -0.5)).astype(BF16)
        qk_ref[:, half:] = y[:, half:].astype(BF16)

    @pl.when(j == 1)
    def _():
        v_ref[...] = y.astype(BF16)

    @pl.when(j == 2)
    def _():
        o_ref[...] = y


def mlstm_in_proj(x, g, w_bf, wg_bf, tm):
    T, D = x.shape
    GW = wg_bf.shape[1]
    return pl.pallas_call(
        _mlin_kernel,
        grid=(T // tm, 3),
        in_specs=[
            pl.BlockSpec((tm, D), lambda i, j: (i, 0)),
            pl.BlockSpec((1, D), lambda i, j: (0, 0)),
            pl.BlockSpec((D, D), lambda i, j: (0, j)),
            pl.BlockSpec((D, GW), lambda i, j: (0, 0)),
        ],
        out_specs=[
            pl.BlockSpec((tm, D), lambda i, j: (i, 0)),
            pl.BlockSpec((tm, D), lambda i, j: (i, 0)),
            pl.BlockSpec((tm, D), lambda i, j: (i, 0)),
            pl.BlockSpec((tm, GW), lambda i, j: (i, 0)),
        ],
        out_shape=(jax.ShapeDtypeStruct((T, D), BF16),
                   jax.ShapeDtypeStruct((T, D), BF16),
                   jax.ShapeDtypeStruct((T, D), F32),
                   jax.ShapeDtypeStruct((T, GW), F32)),
        scratch_shapes=[pltpu.VMEM((tm, D), BF16)],
        compiler_params=_cparams(("arbitrary", "arbitrary")),
        name="mlstm_in_proj",
    )(x, g, w_bf, wg_bf)


def _out_proj_kernel(x_ref, a_ref, w_ref, o_ref):
    o_ref[...] = x_ref[...] + jnp.dot(a_ref[...], w_ref[...], preferred_element_type=F32)


def out_proj(x, a_bf, w_bf, tm):
    T, D = x.shape
    return pl.pallas_call(
        _out_proj_kernel,
        grid=(T // tm,),
        in_specs=[
            pl.BlockSpec((tm, D), lambda i: (i, 0)),
            pl.BlockSpec((tm, D), lambda i: (i, 0)),
            pl.BlockSpec((D, D), lambda i: (0, 0)),
        ],
        out_specs=pl.BlockSpec((tm, D), lambda i: (i, 0)),
        out_shape=jax.ShapeDtypeStruct((T, D), F32),
        compiler_params=_cparams(("arbitrary",)),
        name="out_proj",
    )(x, a_bf, w_bf)


def _diff_lambda(lam_ref, lam_init):
    lp = lam_ref[...]
    a = jnp.sum(lp[0:1] * lp[1:2], axis=-1, keepdims=True)
    b = jnp.sum(lp[2:3] * lp[3:4], axis=-1, keepdims=True)
    return jnp.exp(a) - jnp.exp(b) + lam_init


def _split_maps(q):
    qf = q.astype(F32)
    lane = lax.broadcasted_iota(jnp.int32, qf.shape, 1)
    return (jnp.where(lane < DA_HEAD_DIM, qf, 0.0).astype(BF16),
            jnp.where(lane >= DA_HEAD_DIM, qf, 0.0).astype(BF16))


def _softmax_update(s, v_bf, m_ref, l_ref, acc_ref, idx):
    m_old = m_ref[idx]
    m_new = jnp.maximum(m_old, jnp.max(s, axis=-1, keepdims=True))
    alpha = jnp.exp(m_old - m_new)
    p = jnp.exp(s - m_new)
    l_ref[idx] = alpha * l_ref[idx] + jnp.sum(p, axis=-1, keepdims=True)
    acc_ref[idx] = alpha * acc_ref[idx] + jnp.dot(p.astype(BF16), v_bf, preferred_element_type=F32)
    m_ref[idx] = m_new


def _diff_finish(o1, o2, lam, g, lam_init):
    o = o1 - lam * o2
    return (_rmsnorm_rows(o, g) * (1.0 - lam_init)).astype(BF16)


_NT = (((1,), (1,)), ((), ()))


def _flash_kernel(lam_ref, g_ref, q_ref, k_ref, v_ref, o_ref, vt, m_s, l_s, acc_s, *, tq, lam_init):
    qi = pl.program_id(2)
    n_blk = vt.shape[0]

    @pl.when(qi == 0)
    def _():
        for c in range(n_blk):
            for cc in range(tq // LANES):
                r0 = c * tq + cc * LANES
                vt[c, :, cc * LANES:(cc + 1) * LANES] = v_ref[r0:r0 + LANES, :].astype(F32).T.astype(BF16)

    qm = _split_maps(q_ref[...])
    m_s[...] = jnp.full(m_s.shape, NEG, F32)
    l_s[...] = jnp.zeros(l_s.shape, F32)
    acc_s[...] = jnp.zeros(acc_s.shape, F32)

    def step(ki, masked):
        off = pl.multiple_of(ki * tq, tq)
        kblk = k_ref[pl.ds(off, tq), :]
        vblk = vt[ki]
        scores = [lax.dot_general(kblk, qm[mp], _NT, preferred_element_type=F32) for mp in range(2)]
        for mp in range(2):
            s = scores[mp]
            if masked:
                key = lax.broadcasted_iota(jnp.int32, s.shape, 0)
                qry = lax.broadcasted_iota(jnp.int32, s.shape, 1)
                s = jnp.where(key <= qry, s, NEG)
            m_old = m_s[mp]
            m_new = jnp.maximum(m_old, jnp.max(s, axis=0, keepdims=True))
            alpha = jnp.exp(m_old - m_new)
            p = jnp.exp(s - m_new)
            l_s[mp] = alpha * l_s[mp] + jnp.sum(p, axis=0, keepdims=True)
            acc_s[mp] = alpha * acc_s[mp] + jnp.dot(vblk, p.astype(BF16), preferred_element_type=F32)
            m_s[mp] = m_new

    def body(ki, carry):
        step(ki, False)
        return carry

    lax.fori_loop(0, qi, body, 0)
    step(qi, True)

    lam = _diff_lambda(lam_ref, lam_init)
    o_t = acc_s[0] / l_s[0] - lam * (acc_s[1] / l_s[1])
    o_ref[...] = (_rmsnorm_rows(o_t.T, g_ref[...]) * (1.0 - lam_init)).astype(BF16)


def flash_diff_attention(q_bf, k_f32, v_f32, lam_p, g_subln, batch, seq, lam_init, tq):
    T, D = q_bf.shape
    nq = seq // tq
    kern = functools.partial(_flash_kernel, tq=tq, lam_init=lam_init)
    return pl.pallas_call(
        kern,
        grid=(batch, N_HEADS, nq),
        in_specs=[
            pl.BlockSpec(lam_p.shape, lambda b, h, i: (0, 0)),
            pl.BlockSpec((1, HEAD_W), lambda b, h, i: (0, 0)),
            pl.BlockSpec((tq, HEAD_W), lambda b, h, i: (b * nq + i, h)),
            pl.BlockSpec((seq, HEAD_W), lambda b, h, i: (b, h)),
            pl.BlockSpec((seq, HEAD_W), lambda b, h, i: (b, h)),
        ],
        out_specs=pl.BlockSpec((tq, HEAD_W), lambda b, h, i: (b * nq + i, h)),
        out_shape=jax.ShapeDtypeStruct((T, D), BF16),
        scratch_shapes=[
            pltpu.VMEM((nq, HEAD_W, tq), BF16),
            pltpu.VMEM((2, 1, tq), F32),
            pltpu.VMEM((2, 1, tq), F32),
            pltpu.VMEM((2, HEAD_W, tq), F32),
        ],
        compiler_params=_cparams(("arbitrary", "arbitrary", "arbitrary")),
        name="flash_diff_attention",
    )(lam_p, g_subln, q_bf, k_f32, v_f32)


def _paged_kernel(pt_ref, lam_ref, g_ref, q_ref, kn_ref, vn_ref, *rest, n_pg, lam_init):
    del pt_ref
    k_refs = rest[:n_pg]
    v_refs = rest[n_pg:2 * n_pg]
    o_ref = rest[2 * n_pg]
    qbd, m_s, l_s, acc_s = rest[2 * n_pg + 1:]
    pg = pl.program_id(1)
    dl = q_ref.shape[2]
    page = k_refs[0].shape[1] // N_HEADS

    def head_rows(ref, h, n_tok):
        return ref[0, pl.ds(h, n_tok, stride=N_HEADS), :]

    @pl.when(pg == 0)
    def _():
        for h in range(N_HEADS):
            q = q_ref[0, h]
            lane = lax.broadcasted_iota(jnp.int32, q.shape, 1)
            qbd[h, 0:dl] = jnp.where(lane < DA_HEAD_DIM, q, 0.0).astype(BF16)
            qbd[h, dl:2 * dl] = jnp.where(lane >= DA_HEAD_DIM, q, 0.0).astype(BF16)
        m_s[...] = jnp.full(m_s.shape, NEG, F32)
        l_s[...] = jnp.zeros(l_s.shape, F32)
        acc_s[...] = jnp.zeros(acc_s.shape, F32)

    def all_heads(refs):
        return jnp.stack(
            [jnp.concatenate([head_rows(refs[r], h, page) for r in range(n_pg)], axis=0)
             for h in range(N_HEADS)], axis=0).astype(BF16)

    s = lax.dot_general(qbd[...], all_heads(k_refs), (((2,), (2,)), ((0,), (0,))),
                        preferred_element_type=F32)
    m_old = m_s[...]
    m_new = jnp.maximum(m_old, jnp.max(s, axis=-1, keepdims=True))
    alpha = jnp.exp(m_old - m_new)
    p = jnp.exp(s - m_new)
    l_s[...] = alpha * l_s[...] + jnp.sum(p, axis=-1, keepdims=True)
    pv = lax.dot_general(p.astype(BF16), all_heads(v_refs), (((2,), (1,)), ((0,), (0,))),
                         preferred_element_type=F32)
    acc_s[...] = alpha * acc_s[...] + pv
    m_s[...] = m_new

    @pl.when(pg == pl.num_programs(1) - 1)
    def _():
        lam = _diff_lambda(lam_ref, lam_init)
        for h in range(N_HEADS):
            kn = head_rows(kn_ref, h, dl).astype(BF16)
            vn = head_rows(vn_ref, h, dl).astype(BF16)
            s = lax.dot_general(qbd[h], kn, _NT, preferred_element_type=F32)
            q_tok = lax.broadcasted_iota(jnp.int32, s.shape, 0) & (dl - 1)
            k_tok = lax.broadcasted_iota(jnp.int32, s.shape, 1)
            s = jnp.where(k_tok <= q_tok, s, NEG)
            _softmax_update(s, vn, m_s, l_s, acc_s, h)
            o = acc_s[h] / l_s[h]
            o_ref[0, h] = _diff_finish(o[0:dl], o[dl:2 * dl], lam, g_ref[...], lam_init).astype(F32)


def paged_diff_attention(q_heads, k_new, v_new, cache_k, cache_v, page_table, pool_offset, lam_p, g_subln,
                         lam_init, n_pg):
    DB, _, dl, _ = q_heads.shape
    rows = k_new.shape[1]
    n_pages = page_table.shape[1]
    page_rows = cache_k.shape[1]
    steps = n_pages // n_pg

    def page_idx(r):
        def idx(b, p, pt):
            return (pool_offset + pt[b * n_pages + p * n_pg + r], 0, 0)
        return idx

    seq_idx = lambda b, p, pt: (b, 0, 0)
    head_idx = lambda b, p, pt: (b, 0, 0, 0)
    const2 = lambda b, p, pt: (0, 0)
    in_specs = [
        pl.BlockSpec(lam_p.shape, const2),
        pl.BlockSpec((1, HEAD_W), const2),
        pl.BlockSpec((1, N_HEADS, dl, HEAD_W), head_idx),
        pl.BlockSpec((1, rows, HEAD_W), seq_idx),
        pl.BlockSpec((1, rows, HEAD_W), seq_idx),
    ]
    in_specs += [pl.BlockSpec((1, page_rows, HEAD_W), page_idx(r)) for r in range(n_pg)]
    in_specs += [pl.BlockSpec((1, page_rows, HEAD_W), page_idx(r)) for r in range(n_pg)]
    grid_spec = pltpu.PrefetchScalarGridSpec(
        num_scalar_prefetch=1,
        grid=(DB, steps),
        in_specs=in_specs,
        out_specs=pl.BlockSpec((1, N_HEADS, dl, HEAD_W), head_idx),
        scratch_shapes=[
            pltpu.VMEM((N_HEADS, 2 * dl, HEAD_W), BF16),
            pltpu.VMEM((N_HEADS, 2 * dl, 1), F32),
            pltpu.VMEM((N_HEADS, 2 * dl, 1), F32),
            pltpu.VMEM((N_HEADS, 2 * dl, HEAD_W), F32),
        ],
    )
    kern = functools.partial(_paged_kernel, n_pg=n_pg, lam_init=lam_init)
    return pl.pallas_call(
        kern,
        grid_spec=grid_spec,
        out_shape=jax.ShapeDtypeStruct((DB, N_HEADS, dl, HEAD_W), F32),
        compiler_params=_cparams(("arbitrary", "arbitrary")),
        name="paged_diff_attention",
    )(page_table.reshape(-1), lam_p, g_subln, q_heads, k_new, v_new,
      *([cache_k] * n_pg), *([cache_v] * n_pg))


def _seg_scan(x, pos, seg, op, ident):
    s = 1
    while s < seg:
        shifted = pltpu.roll(x, s, 0)
        x = op(x, jnp.where(pos >= s, shifted, ident))
        s *= 2
    return x


def _seg_last(x, n_seg, seg):
    r, w = x.shape
    x3 = x.reshape(n_seg, seg, w)
    return jnp.broadcast_to(x3[:, seg - 1:seg, :], (n_seg, seg, w)).reshape(r, w)


def _mlstm_kernel(*refs, n_seg, seg, has_state):
    if has_state:
        (qk_ref, v_ref, o_ref, gt_ref, bg_ref, gn_ref, c_in, n_in, m_in,
         h_ref, c_out, n_out, m_out) = refs
    else:
        qk_ref, v_ref, o_ref, gt_ref, bg_ref, gn_ref, h_ref, c_out, n_out, m_out = refs
        c_in, n_in, m_in = c_out, n_out, m_out

        @pl.when(pl.program_id(1) == 0)
        def _():
            c_out[...] = jnp.zeros(c_out.shape, F32)
            n_out[...] = jnp.zeros(n_out.shape, F32)
            m_out[...] = jnp.zeros(m_out.shape, F32)

    R = n_seg * seg
    D = qk_ref.shape[1]
    KOFF = D // 2
    QK = ML_QK_DIM

    pre = gt_ref[...] + bg_ref[...]
    pre = GATE_CAP * jnp.tanh(pre * (1.0 / GATE_CAP))
    li = pre[:, :LANES]
    xf = pre[:, LANES:]
    lf = jnp.minimum(xf, 0.0) - jnp.log1p(jnp.exp(-jnp.abs(xf)))

    rowi = lax.broadcasted_iota(jnp.int32, (R, LANES), 0)
    pos = rowi & (seg - 1)
    bcum = _seg_scan(lf, pos, seg, jnp.add, 0.0)
    a = li - bcum
    cm = _seg_scan(a, pos, seg, jnp.maximum, NEG)
    m0 = m_in[...]
    mx = jnp.maximum(m0, cm)
    mt = bcum + mx
    u = -mx
    w_inter = jnp.exp(m0 + u)
    emt = jnp.exp(-mt)
    b_last = _seg_last(bcum, n_seg, seg)
    m_new = _seg_last(mt, n_seg, seg)
    decay = jnp.exp(b_last + m0 - m_new)
    w_state = jnp.exp(a + b_last - m_new)
    a_t = a.T

    tr = lax.broadcasted_iota(jnp.int32, (R, R), 0)
    tc = lax.broadcasted_iota(jnp.int32, (R, R), 1)
    if n_seg == 1:
        intra = tc <= tr
    else:
        shift = seg.bit_length() - 1
        intra = ((tr >> shift) == (tc >> shift)) & (tc <= tr)

    lane = lax.broadcasted_iota(jnp.int32, (R, LANES), 1)
    low = lane < QK
    hr = lax.broadcasted_iota(jnp.int32, (LANES, LANES), 0)
    hc = lax.broadcasted_iota(jnp.int32, (LANES, LANES), 1)
    half_sel = jnp.where(hc == jnp.where(hr < QK, 0, 1), 1.0, 0.0).astype(BF16)
    low_row = low[0:1]
    if n_seg > 1:
        colseg = lax.broadcasted_iota(jnp.int32, (QK, R), 1) >> (seg.bit_length() - 1)

    for p in range(N_HEADS // 2):
        ha, hb = 2 * p, 2 * p + 1
        qp = qk_ref[:, p * LANES:(p + 1) * LANES].astype(F32)
        kp_bf = qk_ref[:, KOFF + p * LANES:KOFF + (p + 1) * LANES]
        q_heads = (jnp.where(low, qp, 0.0).astype(BF16), jnp.where(low, 0.0, qp).astype(BF16))
        kw = kp_bf.astype(F32) * jnp.where(low, w_state[:, ha:ha + 1], w_state[:, hb:hb + 1])
        kw_t = kw.T
        if n_seg == 1:
            n_rows = n_in[0, p:p + 1, :]
        else:
            n_rows = jnp.concatenate(
                [jnp.broadcast_to(n_in[g, p:p + 1, :], (seg, LANES)) for g in range(n_seg)], axis=0)
        qn = jnp.dot((qp * n_rows).astype(BF16), half_sel, preferred_element_type=F32)
        sp = [c_in[g, ha:hb + 1].reshape(2 * QK, LANES).astype(BF16) for g in range(n_seg)]
        for hi, h in enumerate((ha, hb)):
            qh = q_heads[hi]
            s = lax.dot_general(qh, kp_bf, _NT, preferred_element_type=F32)
            wi = jnp.where(intra, jnp.exp(u[:, h:h + 1] + a_t[h:h + 1, :]), 0.0)
            sw = wi * s
            vh = v_ref[:, h * LANES:(h + 1) * LANES]
            num = jnp.dot(sw.astype(BF16), vh, preferred_element_type=F32)
            den = jnp.sum(sw, axis=-1, keepdims=True)
            if n_seg == 1:
                inter = jnp.dot(qh, sp[0], preferred_element_type=F32)
            else:
                inter = jnp.concatenate(
                    [jnp.dot(qh[g * seg:(g + 1) * seg], sp[g], preferred_element_type=F32)
                     for g in range(n_seg)], axis=0)
            wcol = w_inter[:, h:h + 1]
            num = num + inter * wcol
            den = den + wcol * qn[:, hi:hi + 1]
            denom = jnp.maximum(jnp.abs(den), emt[:, h:h + 1])
            hh = num / denom
            hn = _rmsnorm_rows(hh, gn_ref[:, h * LANES:(h + 1) * LANES])
            og = jax.nn.sigmoid(o_ref[:, h * LANES:(h + 1) * LANES])
            h_ref[:, h * LANES:(h + 1) * LANES] = (og * hn).astype(BF16)
            kw_h = kw_t[hi * QK:(hi + 1) * QK, :]
            if n_seg == 1:
                upd = jnp.dot(kw_h.astype(BF16), vh, preferred_element_type=F32)
                c_out[0, h] = decay[0:1, h:h + 1] * c_in[0, h] + upd
            else:
                lhs = jnp.concatenate(
                    [jnp.where(colseg == g, kw_h, 0.0) for g in range(n_seg)], axis=0).astype(BF16)
                upd = jnp.dot(lhs, vh, preferred_element_type=F32)
                for g in range(n_seg):
                    c_out[g, h] = decay[g * seg:g * seg + 1, h:h + 1] * c_in[g, h] + upd[g * QK:(g + 1) * QK]
        for g in range(n_seg):
            r0 = g * seg
            ksum = jnp.sum(kw[r0:r0 + seg], axis=0, keepdims=True)
            dpair = jnp.where(low_row, decay[r0:r0 + 1, ha:ha + 1], decay[r0:r0 + 1, hb:hb + 1])
            n_out[g, p:p + 1, :] = dpair * n_in[g, p:p + 1, :] + ksum
    m_out[...] = m_new


def mlstm_prompt(qk, v, o, gates, bgate, gnorm, batch, seq):
    T, D = qk.shape
    R = LANES
    nc = seq // R
    GW = gates.shape[1]
    row = lambda b, c: (b * nc + c, 0)
    const = lambda b, c: (0, 0)
    st = lambda b, c: (b, 0, 0, 0)
    kern = functools.partial(_mlstm_kernel, n_seg=1, seg=R, has_state=False)
    return pl.pallas_call(
        kern,
        grid=(batch, nc),
        in_specs=[
            pl.BlockSpec((R, D), row), pl.BlockSpec((R, D), row), pl.BlockSpec((R, D), row),
            pl.BlockSpec((R, GW), row), pl.BlockSpec((1, GW), const), pl.BlockSpec((1, D), const),
        ],
        out_specs=[
            pl.BlockSpec((R, D), row),
            pl.BlockSpec((1, N_HEADS, ML_QK_DIM, LANES), st),
            pl.BlockSpec((1, N_HEADS // 2, LANES), lambda b, c: (b, 0, 0)),
            pl.BlockSpec((R, LANES), lambda b, c: (b, 0)),
        ],
        out_shape=(jax.ShapeDtypeStruct((T, D), BF16),
                   jax.ShapeDtypeStruct((batch, N_HEADS, ML_QK_DIM, LANES), F32),
                   jax.ShapeDtypeStruct((batch, N_HEADS // 2, LANES), F32),
                   jax.ShapeDtypeStruct((batch * R, LANES), F32)),
        compiler_params=_cparams(("arbitrary", "arbitrary")),
        name="mlstm_prompt",
    )(qk, v, o, gates, bgate, gnorm)


def mlstm_sample(qk, v, o, gates, bgate, gnorm, c0, n0_pairs, m0_rows, seg):
    T, D = qk.shape
    R = LANES
    n_seg = R // seg
    GW = gates.shape[1]
    row = lambda i: (i, 0)
    const = lambda i: (0, 0)
    st = lambda i: (i, 0, 0, 0)
    st_block = (n_seg, N_HEADS, ML_QK_DIM, LANES)
    n_block = (n_seg, N_HEADS // 2, LANES)
    nst = lambda i: (i, 0, 0)
    kern = functools.partial(_mlstm_kernel, n_seg=n_seg, seg=seg, has_state=True)
    return pl.pallas_call(
        kern,
        grid=(T // R,),
        in_specs=[
            pl.BlockSpec((R, D), row), pl.BlockSpec((R, D), row), pl.BlockSpec((R, D), row),
            pl.BlockSpec((R, GW), row), pl.BlockSpec((1, GW), const), pl.BlockSpec((1, D), const),
            pl.BlockSpec(st_block, st), pl.BlockSpec(n_block, nst), pl.BlockSpec((R, LANES), row),
        ],
        out_specs=[
            pl.BlockSpec((R, D), row),
            pl.BlockSpec(st_block, st),
            pl.BlockSpec(n_block, nst),
            pl.BlockSpec((R, LANES), row),
        ],
        out_shape=(jax.ShapeDtypeStruct((T, D), BF16),
                   jax.ShapeDtypeStruct(c0.shape, F32),
                   jax.ShapeDtypeStruct(n0_pairs.shape, F32),
                   jax.ShapeDtypeStruct((T, LANES), F32)),
        compiler_params=_cparams(("arbitrary",)),
        name="mlstm_sample",
    )(qk, v, o, gates, bgate, gnorm, c0, n0_pairs, m0_rows)


def _ffn_core(x_ref, gn_ref, wg_ref, wu_ref, wd_ref, cw_ref, y_ref, hn_ref, shifted, final_g_ref):
    f = pl.program_id(1)

    @pl.when(f == 0)
    def _():
        _norm_into(hn_ref, x_ref, gn_ref)
        y_ref[...] = x_ref[...]

    hn = hn_ref[...]
    g = jnp.dot(hn, wg_ref[...], preferred_element_type=F32)
    u = jnp.dot(hn, wu_ref[...], preferred_element_type=F32)
    g1, g2 = shifted(g)
    cw = cw_ref[...]
    gc = cw[3:4] + cw[0:1] * g2 + cw[1:2] * g1 + cw[2:3] * g
    act = (jax.nn.silu(gc) * u).astype(BF16)
    y_ref[...] += jnp.dot(act, wd_ref[...], preferred_element_type=F32)

    if final_g_ref is not None:
        @pl.when(f == pl.num_programs(1) - 1)
        def _():
            y_ref[...] = _rmsnorm_rows(y_ref[...], final_g_ref[...])
    return g


def _ffn_prompt_kernel(*refs, tiles_per_seq, final_norm):
    if final_norm:
        x_ref, gn_ref, wg_ref, wu_ref, wd_ref, cw_ref, fg_ref, y_ref, st_ref, hn_ref, gbuf, carry = refs
    else:
        x_ref, gn_ref, wg_ref, wu_ref, wd_ref, cw_ref, y_ref, st_ref, hn_ref, gbuf, carry = refs
        fg_ref = None
    i = pl.program_id(0)
    f = pl.program_id(1)
    tm = x_ref.shape[0]

    @pl.when((i == 0) & (f == 0))
    def _():
        carry[...] = jnp.zeros(carry.shape, F32)

    def shifted(g):
        prev = jnp.where(i % tiles_per_seq == 0, 0.0, carry[f])
        gbuf[0:SUBLANES] = prev
        gbuf[SUBLANES:] = g
        return gbuf[SUBLANES - 1:SUBLANES - 1 + tm], gbuf[SUBLANES - 2:SUBLANES - 2 + tm]

    g = _ffn_core(x_ref, gn_ref, wg_ref, wu_ref, wd_ref, cw_ref, y_ref, hn_ref, shifted, fg_ref)
    tail = g[tm - SUBLANES:]
    carry[f] = tail
    st_ref[0] = tail


def _ffn_sample_kernel(*refs, seg, final_norm):
    if final_norm:
        x_ref, gn_ref, wg_ref, wu_ref, wd_ref, cw_ref, e_ref, fg_ref, y_ref, g_ref, hn_ref = refs
    else:
        x_ref, gn_ref, wg_ref, wu_ref, wd_ref, cw_ref, e_ref, y_ref, g_ref, hn_ref = refs
        fg_ref = None
    tm = x_ref.shape[0]

    def shifted(g):
        e = e_ref[...]
        pos = lax.broadcasted_iota(jnp.int32, g.shape, 0) & (seg - 1)
        g1 = jnp.where(pos < 1, pltpu.roll(e, tm - 1, 0), pltpu.roll(g, 1, 0))
        g2 = jnp.where(pos < 2, e, pltpu.roll(g, 2, 0))
        return g1, g2

    g = _ffn_core(x_ref, gn_ref, wg_ref, wu_ref, wd_ref, cw_ref, y_ref, hn_ref, shifted, fg_ref)
    g_ref[...] = g


def _ffn_common_specs(D, tm, tf, n_f):
    return [
        pl.BlockSpec((tm, D), lambda i, f: (i, 0)),
        pl.BlockSpec((1, D), lambda i, f: (0, 0)),
        pl.BlockSpec((D, tf), lambda i, f: (0, f)),
        pl.BlockSpec((D, tf), lambda i, f: (0, n_f + f)),
        pl.BlockSpec((tf, D), lambda i, f: (f, 0)),
        pl.BlockSpec((SUBLANES, tf), lambda i, f: (0, f)),
    ]


def ffn_prompt(x, gn, w_up_bf, w_down_bf, cw, batch, seq, tm, tf, final_g=None):
    T, D = x.shape
    F = w_down_bf.shape[0]
    n_f = F // tf
    tps = seq // tm
    in_specs = _ffn_common_specs(D, tm, tf, n_f)
    args = [x, gn, w_up_bf, w_up_bf, w_down_bf, cw]
    if final_g is not None:
        in_specs.append(pl.BlockSpec((1, D), lambda i, f: (0, 0)))
        args.append(final_g)
    kern = functools.partial(_ffn_prompt_kernel, tiles_per_seq=tps, final_norm=final_g is not None)
    return pl.pallas_call(
        kern,
        grid=(T // tm, n_f),
        in_specs=in_specs,
        out_specs=[
            pl.BlockSpec((tm, D), lambda i, f: (i, 0)),
            pl.BlockSpec((1, SUBLANES, tf), lambda i, f: (i, 0, f)),
        ],
        out_shape=(jax.ShapeDtypeStruct((T, D), F32),
                   jax.ShapeDtypeStruct((T // tm, SUBLANES, F), F32)),
        scratch_shapes=[
            pltpu.VMEM((tm, D), BF16),
            pltpu.VMEM((tm + SUBLANES, tf), F32),
            pltpu.VMEM((n_f, SUBLANES, tf), F32),
        ],
        compiler_params=_cparams(("arbitrary", "arbitrary")),
        name="ffn_prompt",
    )(*args)


def ffn_sample(x, gn, w_up_bf, w_down_bf, cw, e_rows, seg, tm, tf, final_g=None):
    T, D = x.shape
    F = w_down_bf.shape[0]
    n_f = F // tf
    in_specs = _ffn_common_specs(D, tm, tf, n_f)
    in_specs.append(pl.BlockSpec((tm, tf), lambda i, f: (i, f)))
    args = [x, gn, w_up_bf, w_up_bf, w_down_bf, cw, e_rows]
    if final_g is not None:
        in_specs.append(pl.BlockSpec((1, D), lambda i, f: (0, 0)))
        args.append(final_g)
    kern = functools.partial(_ffn_sample_kernel, seg=seg, final_norm=final_g is not None)
    return pl.pallas_call(
        kern,
        grid=(T // tm, n_f),
        in_specs=in_specs,
        out_specs=[
            pl.BlockSpec((tm, D), lambda i, f: (i, 0)),
            pl.BlockSpec((tm, tf), lambda i, f: (i, f)),
        ],
        out_shape=(jax.ShapeDtypeStruct((T, D), F32),
                   jax.ShapeDtypeStruct((T, F), F32)),
        scratch_shapes=[pltpu.VMEM((tm, D), BF16)],
        compiler_params=_cparams(("arbitrary", "arbitrary")),
        name="ffn_sample",
    )(*args)


def kernel(x_prompt, x_sample, cache_k, cache_v, page_table, state_mlstm_c, state_mlstm_n, state_mlstm_m,
           state_conv, g_mix_norm, g_ffn_norm, g_final, attn_w_qkv, attn_lambda, attn_subln, attn_w_o,
           ml_w_in, ml_b_gate, ml_norm, ml_w_out, ffn_w_up, ffn_conv_w, ffn_conv_b, ffn_w_down):
    B, L, D = x_prompt.shape
    DB, DL, _ = x_sample.shape
    depth = g_mix_norm.shape[0]
    n_attn, n_pool = cache_k.shape[:2]
    n_pages = page_table.shape[1]
    past = n_pages * PAGE_SIZE
    d_ff = ffn_w_down.shape[1]
    n_ml_heads = state_mlstm_c.shape[2]
    assert D == N_HEADS * HEAD_W and n_ml_heads == N_HEADS and DL == SUBLANES
    assert cache_k.shape[2:] == (PAGE_SIZE, N_HEADS, HEAD_W) and cache_v.shape[2:] == (PAGE_SIZE, N_HEADS, HEAD_W)

    TP, TS = B * L, DB * DL
    TM = 512
    TM_S = min(TM, TS)
    TM_FFN = 1024
    TM_FFN_S = min(TM_FFN, TS)
    TF = 256
    TQ = 512
    N_PG = 4

    xp = x_prompt.reshape(TP, D)
    xs = x_sample.reshape(TS, D)
    rope_p, rope_s = rope_tables(L, past, DL, TM_S)
    ck = cache_k.reshape(n_attn * n_pool, PAGE_SIZE * N_HEADS, HEAD_W)
    cv = cache_v.reshape(n_attn * n_pool, PAGE_SIZE * N_HEADS, HEAD_W)

    qk_w = N_HEADS * ML_QK_DIM
    k_p, v_p, k_s, v_s = [], [], [], []
    c_p, n_p, m_p, c_s, n_s, m_s = [], [], [], [], [], []
    cv_p, cv_s = [], []
    for i in range(depth):
        j = i // 2
        g_mix = g_mix_norm[i].reshape(1, D)
        if i % 2 == 0:
            lam_init = 0.8 - 0.6 * math.exp(-0.3 * i)
            w_qkv = attn_w_qkv[j].astype(BF16)
            w_o = attn_w_o[j].astype(BF16)
            g_sub = attn_subln[j].reshape(1, HEAD_W)
            lam_p = attn_lambda[j]
            qp, kbp, vbp, kp, vp = qkv_proj(xp, g_mix, w_qkv, rope_p, TM, L // TM)
            qs, _, _, ks, vs = qkv_proj(xs, g_mix, w_qkv, rope_s, TM_S, 1)
            ap = flash_diff_attention(qp, kbp, vbp, lam_p, g_sub, B, L, lam_init, TQ)
            rows = DL * N_HEADS
            q_heads = jnp.transpose(qs.reshape(DB, DL, N_HEADS, HEAD_W), (0, 2, 1, 3)).astype(F32)
            as_ = paged_diff_attention(
                q_heads, ks.reshape(DB, rows, HEAD_W), vs.reshape(DB, rows, HEAD_W),
                ck, cv, page_table, j * n_pool, lam_p, g_sub, lam_init, N_PG)
            as_ = jnp.transpose(as_, (0, 2, 1, 3)).reshape(TS, D).astype(BF16)
            xp = out_proj(xp, ap, w_o, TM)
            xs = out_proj(xs, as_, w_o, TM_S)
            k_p.append(kp.reshape(B, L, N_HEADS, HEAD_W))
            v_p.append(vp.reshape(B, L, N_HEADS, HEAD_W))
            k_s.append(ks.reshape(DB, DL, N_HEADS, HEAD_W))
            v_s.append(vs.reshape(DB, DL, N_HEADS, HEAD_W))
        else:
            w_in = ml_w_in[j]
            w_main = w_in[:, :2 * qk_w + 2 * D].astype(BF16)
            w_gate = w_in[:, 2 * qk_w + 2 * D:]
            pad = jnp.zeros((D, LANES - N_HEADS), F32)
            wg = jnp.concatenate([w_gate[:, :N_HEADS], pad, w_gate[:, N_HEADS:], pad], axis=1).astype(BF16)
            bg = ml_b_gate[j]
            padb = jnp.zeros((LANES - N_HEADS,), F32)
            bgate = jnp.concatenate([bg[:N_HEADS], padb, bg[N_HEADS:], padb]).reshape(1, 2 * LANES)
            gnorm = ml_norm[j].reshape(1, D)
            w_out = ml_w_out[j].astype(BF16)
            qkp, vvp, oop, gtp = mlstm_in_proj(xp, g_mix, w_main, wg, TM)
            qks, vvs, oos, gts = mlstm_in_proj(xs, g_mix, w_main, wg, TM_S)
            hp, cp_, np_, mp_ = mlstm_prompt(qkp, vvp, oop, gtp, bgate, gnorm, B, L)
            n0_pairs = state_mlstm_n[j].reshape(DB, N_HEADS // 2, LANES)
            m0_rows = jnp.repeat(jnp.pad(state_mlstm_m[j], ((0, 0), (0, LANES - N_HEADS))), DL, axis=0)
            hs, cs_, ns_, ms_ = mlstm_sample(qks, vvs, oos, gts, bgate, gnorm,
                                             state_mlstm_c[j], n0_pairs, m0_rows, DL)
            xp = out_proj(xp, hp, w_out, TM)
            xs = out_proj(xs, hs, w_out, TM_S)
            c_p.append(cp_)
            n_p.append(np_.reshape(B, N_HEADS, ML_QK_DIM))
            m_p.append(mp_.reshape(B, LANES, LANES)[:, 0, :N_HEADS])
            c_s.append(cs_)
            n_s.append(ns_.reshape(DB, N_HEADS, ML_QK_DIM))
            m_s.append(ms_.reshape(DB, DL, LANES)[:, DL - 1, :N_HEADS])
        g_ffn = g_ffn_norm[i].reshape(1, D)
        w_up = ffn_w_up[i].astype(BF16)
        w_down = ffn_w_down[i].astype(BF16)
        cw = jnp.concatenate([ffn_conv_w[i], ffn_conv_b[i][None, :],
                              jnp.zeros((SUBLANES - CONV_W - 1, d_ff), F32)], axis=0)
        fg = g_final.reshape(1, D) if i == depth - 1 else None
        xp, stp = ffn_prompt(xp, g_ffn, w_up, w_down, cw, B, L, TM_FFN, TF, fg)
        e_rows = jnp.pad(state_conv[i], ((0, 0), (0, DL - (CONV_W - 1)), (0, 0))).reshape(TS, d_ff)
        xs, gs = ffn_sample(xs, g_ffn, w_up, w_down, cw, e_rows, DL, TM_FFN_S, TF, fg)
        cv_p.append(stp.reshape(B, L // TM_FFN, SUBLANES, d_ff)[:, -1, SUBLANES - (CONV_W - 1):, :])
        cv_s.append(gs.reshape(DB, DL, d_ff)[:, DL - (CONV_W - 1):, :])

    return (xp.reshape(B, L, D), xs.reshape(DB, DL, D),
            jnp.stack(k_p), jnp.stack(v_p), jnp.stack(k_s), jnp.stack(v_s),
            jnp.stack(c_p), jnp.stack(n_p), jnp.stack(m_p),
            jnp.stack(c_s), jnp.stack(n_s), jnp.stack(m_s),
            jnp.stack(cv_p), jnp.stack(cv_s))
```

```python
import functools
import math

import jax
import jax.numpy as jnp
from jax import lax
from jax.experimental import pallas as pl
from jax.experimental.pallas import tpu as pltpu

F32 = jnp.float32
BF16 = jnp.bfloat16

LANES = 128
SUBLANES = 8
VMEM_LIMIT = 56 * 1024 * 1024

EPS = 1e-6
ROPE_THETA = 10000.0
PAGE_SIZE = 128
N_HEADS = 8
HEAD_W = 128
DA_HEAD_DIM = 64
ML_QK_DIM = 64
GATE_CAP = 15.0
CONV_W = 3
NEG = -1e30


def _cparams(sem):
    return pltpu.CompilerParams(dimension_semantics=sem, vmem_limit_bytes=VMEM_LIMIT)


def _rmsnorm_rows(x, g):
    return x * lax.rsqrt(jnp.mean(x * x, axis=-1, keepdims=True) + EPS) * g


def _rope_table_kernel(tp_ref, ts_ref, *, past, dec_seq):
    def tables(pos, shape):
        lane = lax.broadcasted_iota(jnp.int32, shape, 1)
        half = DA_HEAD_DIM // 2
        fidx = (lane & (half - 1)).astype(F32)
        freq = jnp.exp(fidx * (-math.log(ROPE_THETA) / half))
        ang = pos * freq
        c = jnp.cos(ang)
        s = jnp.sin(ang)
        first = (lane & (DA_HEAD_DIM - 1)) < half
        return c, jnp.where(first, -s, 0.0), jnp.where(first, 0.0, s)

    shp = tp_ref.shape[1:]
    pos_p = lax.broadcasted_iota(jnp.int32, shp, 0).astype(F32)
    c, s1, s2 = tables(pos_p, shp)
    tp_ref[0] = c
    tp_ref[1] = s1
    tp_ref[2] = s2
    shs = ts_ref.shape[1:]
    row = lax.broadcasted_iota(jnp.int32, shs, 0)
    pos_s = (past + (row & (dec_seq - 1))).astype(F32)
    c, s1, s2 = tables(pos_s, shs)
    ts_ref[0] = c
    ts_ref[1] = s1
    ts_ref[2] = s2


def rope_tables(seq, past, dec_seq, sample_rows):
    return pl.pallas_call(
        functools.partial(_rope_table_kernel, past=past, dec_seq=dec_seq),
        out_shape=(jax.ShapeDtypeStruct((3, seq, LANES), F32),
                   jax.ShapeDtypeStruct((3, sample_rows, LANES), F32)),
        name="rope_tables",
    )()


def _norm_into(hn_ref, x_ref, g_ref):
    hn_ref[...] = _rmsnorm_rows(x_ref[...], g_ref[...]).astype(BF16)


def _rope_head(y, c, s1, s2):
    return y * c + pltpu.roll(y, LANES - 32, 1) * s1 + pltpu.roll(y, 32, 1) * s2


def _qkv_kernel(x_ref, g_ref, w_ref, rope_ref, q_ref, kb_ref, vb_ref, k3_ref, v3_ref, hn_ref, tr_ref):
    j = pl.program_id(1)
    tm = x_ref.shape[0]

    @pl.when(j == 0)
    def _():
        _norm_into(hn_ref, x_ref, g_ref)

    y = jnp.dot(hn_ref[...], w_ref[...], preferred_element_type=F32)

    @pl.when(j == 0)
    def _():
        c, s1, s2 = rope_ref[0], rope_ref[1], rope_ref[2]
        for h in range(N_HEADS):
            sl = slice(h * HEAD_W, (h + 1) * HEAD_W)
            q_ref[:, sl] = (_rope_head(y[:, sl], c, s1, s2) * (DA_HEAD_DIM ** -0.5)).astype(BF16)

    def head_tile(h):
        return slice(h * SUBLANES, (h + 1) * SUBLANES)

    def to_cache_rows(out_ref):
        for s in range(SUBLANES):
            out_ref[:, s] = tr_ref[:, pl.ds(s, N_HEADS, stride=SUBLANES), :]

    @pl.when(j == 1)
    def _():
        c, s1, s2 = rope_ref[0], rope_ref[1], rope_ref[2]
        for h in range(N_HEADS):
            sl = slice(h * HEAD_W, (h + 1) * HEAD_W)
            kh = _rope_head(y[:, sl], c, s1, s2)
            kb_ref[:, sl] = kh.astype(BF16)
            tr_ref[:, head_tile(h), :] = kh.reshape(tm // SUBLANES, SUBLANES, HEAD_W)
        to_cache_rows(k3_ref)

    @pl.when(j == 2)
    def _():
        vb_ref[...] = y.astype(BF16)
        for h in range(N_HEADS):
            tr_ref[:, head_tile(h), :] = y[:, h * HEAD_W:(h + 1) * HEAD_W].reshape(
                tm // SUBLANES, SUBLANES, HEAD_W)
        to_cache_rows(v3_ref)


def qkv_proj(x, g, w_bf, rope_tab, tm, rope_blocks):
    T, D = x.shape
    n_i = T // tm

    def rope_idx(i, j):
        return (0, i % rope_blocks, 0)

    flat = pl.BlockSpec((tm, D), lambda i, j: (i, 0))
    tg = tm // SUBLANES
    rows3 = pl.BlockSpec((tg, SUBLANES, N_HEADS, HEAD_W), lambda i, j: (i, 0, 0, 0))
    return pl.pallas_call(
        _qkv_kernel,
        grid=(n_i, 3),
        in_specs=[
            flat,
            pl.BlockSpec((1, D), lambda i, j: (0, 0)),
            pl.BlockSpec((D, D), lambda i, j: (0, j)),
            pl.BlockSpec((3, tm, LANES), rope_idx),
        ],
        out_specs=[flat, flat, flat, rows3, rows3],
        out_shape=(jax.ShapeDtypeStruct((T, D), BF16),
                   jax.ShapeDtypeStruct((T, D), BF16),
                   jax.ShapeDtypeStruct((T, D), BF16),
                   jax.ShapeDtypeStruct((T // SUBLANES, SUBLANES, N_HEADS, HEAD_W), F32),
                   jax.ShapeDtypeStruct((T // SUBLANES, SUBLANES, N_HEADS, HEAD_W), F32)),
        scratch_shapes=[pltpu.VMEM((tm, D), BF16),
                        pltpu.VMEM((tg, N_HEADS * SUBLANES, HEAD_W), F32)],
        compiler_params=_cparams(("arbitrary", "arbitrary")),
        name="qkv_proj",
    )(x, g, w_bf, rope_tab)


def _mlin_kernel(x_ref, g_ref, w_ref, wg_ref, qk_ref, v_ref, o_ref, gt_ref, hn_ref):
    j = pl.program_id(1)

    @pl.when(j == 0)
    def _():
        _norm_into(hn_ref, x_ref, g_ref)
        gt_ref[...] = jnp.dot(hn_ref[...], wg_ref[...], preferred_element_type=F32)

    y = jnp.dot(hn_ref[...], w_ref[...], preferred_element_type=F32)

    @pl.when(j == 0)
    def _():
        half = y.shape[1] // 2
        qk_ref[:, :half] = (y[:, :half] * (ML_QK_DIM ** -0.5)).astype(BF16)
        qk_ref[:, half:] = y[:, half:].astype(BF16)

    @pl.when(j == 1)
    def _():
        v_ref[...] = y.astype(BF16)

    @pl.when(j == 2)
    def _():
        o_ref[...] = y


def mlstm_in_proj(x, g, w_bf, wg_bf, tm):
    T, D = x.shape
    GW = wg_bf.shape[1]
    return pl.pallas_call(
        _mlin_kernel,
        grid=(T // tm, 3),
        in_specs=[
            pl.BlockSpec((tm, D), lambda i, j: (i, 0)),
            pl.BlockSpec((1, D), lambda i, j: (0, 0)),
            pl.BlockSpec((D, D), lambda i, j: (0, j)),
            pl.BlockSpec((D, GW), lambda i, j: (0, 0)),
        ],
        out_specs=[
            pl.BlockSpec((tm, D), lambda i, j: (i, 0)),
            pl.BlockSpec((tm, D), lambda i, j: (i, 0)),
            pl.BlockSpec((tm, D), lambda i, j: (i, 0)),
            pl.BlockSpec((tm, GW), lambda i, j: (i, 0)),
        ],
        out_shape=(jax.ShapeDtypeStruct((T, D), BF16),
                   jax.ShapeDtypeStruct((T, D), BF16),
                   jax.ShapeDtypeStruct((T, D), F32),
                   jax.ShapeDtypeStruct((T, GW), F32)),
        scratch_shapes=[pltpu.VMEM((tm, D), BF16)],
        compiler_params=_cparams(("arbitrary", "arbitrary")),
        name="mlstm_in_proj",
    )(x, g, w_bf, wg_bf)


def _diff_lambda(lam_ref, lam_init):
    lp = lam_ref[...]
    a = jnp.sum(lp[0:1] * lp[1:2], axis=-1, keepdims=True)
    b = jnp.sum(lp[2:3] * lp[3:4], axis=-1, keepdims=True)
    return jnp.exp(a) - jnp.exp(b) + lam_init


def _split_maps(q):
    qf = q.astype(F32)
    lane = lax.broadcasted_iota(jnp.int32, qf.shape, 1)
    return (jnp.where(lane < DA_HEAD_DIM, qf, 0.0).astype(BF16),
            jnp.where(lane >= DA_HEAD_DIM, qf, 0.0).astype(BF16))


def _softmax_update(s, v_bf, m_ref, l_ref, acc_ref, idx):
    m_old = m_ref[idx]
    m_new = jnp.maximum(m_old, jnp.max(s, axis=-1, keepdims=True))
    alpha = jnp.exp(m_old - m_new)
    p = jnp.exp(s - m_new)
    l_ref[idx] = alpha * l_ref[idx] + jnp.sum(p, axis=-1, keepdims=True)
    acc_ref[idx] = alpha * acc_ref[idx] + jnp.dot(p.astype(BF16), v_bf, preferred_element_type=F32)
    m_ref[idx] = m_new


def _diff_finish(o1, o2, lam, g, lam_init):
    o = o1 - lam * o2
    return (_rmsnorm_rows(o, g) * (1.0 - lam_init)).astype(BF16)


_NT = (((1,), (1,)), ((), ()))


def _flash_kernel(lam_ref, g_ref, q_ref, k_ref, v_ref, o_ref, vt, m_s, l_s, acc_s, *, tq, lam_init):
    qi = pl.program_id(2)
    n_blk = vt.shape[0]

    @pl.when(qi == 0)
    def _():
        for c in range(n_blk):
            for cc in range(tq // LANES):
                r0 = c * tq + cc * LANES
                vt[c, :, cc * LANES:(cc + 1) * LANES] = v_ref[r0:r0 + LANES, :].astype(F32).T.astype(BF16)

    qm = _split_maps(q_ref[...])
    m_s[...] = jnp.full(m_s.shape, NEG, F32)
    l_s[...] = jnp.zeros(l_s.shape, F32)
    acc_s[...] = jnp.zeros(acc_s.shape, F32)

    def step(ki, masked):
        off = pl.multiple_of(ki * tq, tq)
        kblk = k_ref[pl.ds(off, tq), :]
        vblk = vt[ki]
        scores = [lax.dot_general(kblk, qm[mp], _NT, preferred_element_type=F32) for mp in range(2)]
        for mp in range(2):
            s = scores[mp]
            if masked:
                key = lax.broadcasted_iota(jnp.int32, s.shape, 0)
                qry = lax.broadcasted_iota(jnp.int32, s.shape, 1)
                s = jnp.where(key <= qry, s, NEG)
            m_old = m_s[mp]
            m_new = jnp.maximum(m_old, jnp.max(s, axis=0, keepdims=True))
            alpha = jnp.exp(m_old - m_new)
            p = jnp.exp(s - m_new)
            l_s[mp] = alpha * l_s[mp] + jnp.sum(p, axis=0, keepdims=True)
            acc_s[mp] = alpha * acc_s[mp] + jnp.dot(vblk, p.astype(BF16), preferred_element_type=F32)
            m_s[mp] = m_new

    def body(ki, carry):
        step(ki, False)
        return carry

    lax.fori_loop(0, qi, body, 0)
    step(qi, True)

    lam = _diff_lambda(lam_ref, lam_init)
    o_t = acc_s[0] / l_s[0] - lam * (acc_s[1] / l_s[1])
    o_ref[...] = (_rmsnorm_rows(o_t.T, g_ref[...]) * (1.0 - lam_init)).astype(BF16)


def flash_diff_attention(q_bf, k_f32, v_f32, lam_p, g_subln, batch, seq, lam_init, tq):
    T, D = q_bf.shape
    nq = seq // tq
    kern = functools.partial(_flash_kernel, tq=tq, lam_init=lam_init)
    return pl.pallas_call(
        kern,
        grid=(batch, N_HEADS, nq),
        in_specs=[
            pl.BlockSpec(lam_p.shape, lambda b, h, i: (0, 0)),
            pl.BlockSpec((1, HEAD_W), lambda b, h, i: (0, 0)),
            pl.BlockSpec((tq, HEAD_W), lambda b, h, i: (b * nq + i, h)),
            pl.BlockSpec((seq, HEAD_W), lambda b, h, i: (b, h)),
            pl.BlockSpec((seq, HEAD_W), lambda b, h, i: (b, h)),
        ],
        out_specs=pl.BlockSpec((tq, HEAD_W), lambda b, h, i: (b * nq + i, h)),
        out_shape=jax.ShapeDtypeStruct((T, D), BF16),
        scratch_shapes=[
            pltpu.VMEM((nq, HEAD_W, tq), BF16),
            pltpu.VMEM((2, 1, tq), F32),
            pltpu.VMEM((2, 1, tq), F32),
            pltpu.VMEM((2, HEAD_W, tq), F32),
        ],
        compiler_params=_cparams(("arbitrary", "arbitrary", "arbitrary")),
        name="flash_diff_attention",
    )(lam_p, g_subln, q_bf, k_f32, v_f32)


def _paged_kernel(pt_ref, lam_ref, g_ref, q_ref, kn_ref, vn_ref, *rest, n_pg, lam_init):
    del pt_ref
    k_refs = rest[:n_pg]
    v_refs = rest[n_pg:2 * n_pg]
    o_ref = rest[2 * n_pg]
    qbd, m_s, l_s, acc_s = rest[2 * n_pg + 1:]
    pg = pl.program_id(1)
    dl = q_ref.shape[2]
    page = k_refs[0].shape[1] // N_HEADS

    def head_rows(ref, h, n_tok):
        return ref[0, pl.ds(h, n_tok, stride=N_HEADS), :]

    @pl.when(pg == 0)
    def _():
        for h in range(N_HEADS):
            q = q_ref[0, h]
            lane = lax.broadcasted_iota(jnp.int32, q.shape, 1)
            qbd[h, 0:dl] = jnp.where(lane < DA_HEAD_DIM, q, 0.0).astype(BF16)
            qbd[h, dl:2 * dl] = jnp.where(lane >= DA_HEAD_DIM, q, 0.0).astype(BF16)
        m_s[...] = jnp.full(m_s.shape, NEG, F32)
        l_s[...] = jnp.zeros(l_s.shape, F32)
        acc_s[...] = jnp.zeros(acc_s.shape, F32)

    def all_heads(refs):
        return jnp.stack(
            [jnp.concatenate([head_rows(refs[r], h, page) for r in range(n_pg)], axis=0)
             for h in range(N_HEADS)], axis=0).astype(BF16)

    s = lax.dot_general(qbd[...], all_heads(k_refs), (((2,), (2,)), ((0,), (0,))),
                        preferred_element_type=F32)
    m_old = m_s[...]
    m_new = jnp.maximum(m_old, jnp.max(s, axis=-1, keepdims=True))
    alpha = jnp.exp(m_old - m_new)
    p = jnp.exp(s - m_new)
    l_s[...] = alpha * l_s[...] + jnp.sum(p, axis=-1, keepdims=True)
    pv = lax.dot_general(p.astype(BF16), all_heads(v_refs), (((2,), (1,)), ((0,), (0,))),
                         preferred_element_type=F32)
    acc_s[...] = alpha * acc_s[...] + pv
    m_s[...] = m_new

    @pl.when(pg == pl.num_programs(1) - 1)
    def _():
        lam = _diff_lambda(lam_ref, lam_init)
        for h in range(N_HEADS):
            kn = head_rows(kn_ref, h, dl).astype(BF16)
            vn = head_rows(vn_ref, h, dl).astype(BF16)
            s = lax.dot_general(qbd[h], kn, _NT, preferred_element_type=F32)
            q_tok = lax.broadcasted_iota(jnp.int32, s.shape, 0) & (dl - 1)
            k_tok = lax.broadcasted_iota(jnp.int32, s.shape, 1)
            s = jnp.where(k_tok <= q_tok, s, NEG)
            _softmax_update(s, vn, m_s, l_s, acc_s, h)
            o = acc_s[h] / l_s[h]
            o_ref[0, h] = _diff_finish(o[0:dl], o[dl:2 * dl], lam, g_ref[...], lam_init).astype(F32)


def paged_diff_attention(q_heads, k_new, v_new, cache_k, cache_v, page_table, pool_offset, lam_p, g_subln,
                         lam_init, n_pg):
    DB, _, dl, _ = q_heads.shape
    rows = k_new.shape[1]
    n_pages = page_table.shape[1]
    page_rows = cache_k.shape[1]
    steps = n_pages // n_pg

    def page_idx(r):
        def idx(b, p, pt):
            return (pool_offset + pt[b * n_pages + p * n_pg + r], 0, 0)
        return idx

    seq_idx = lambda b, p, pt: (b, 0, 0)
    head_idx = lambda b, p, pt: (b, 0, 0, 0)
    const2 = lambda b, p, pt: (0, 0)
    in_specs = [
        pl.BlockSpec(lam_p.shape, const2),
        pl.BlockSpec((1, HEAD_W), const2),
        pl.BlockSpec((1, N_HEADS, dl, HEAD_W), head_idx),
        pl.BlockSpec((1, rows, HEAD_W), seq_idx),
        pl.BlockSpec((1, rows, HEAD_W), seq_idx),
    ]
    in_specs += [pl.BlockSpec((1, page_rows, HEAD_W), page_idx(r)) for r in range(n_pg)]
    in_specs += [pl.BlockSpec((1, page_rows, HEAD_W), page_idx(r)) for r in range(n_pg)]
    grid_spec = pltpu.PrefetchScalarGridSpec(
        num_scalar_prefetch=1,
        grid=(DB, steps),
        in_specs=in_specs,
        out_specs=pl.BlockSpec((1, N_HEADS, dl, HEAD_W), head_idx),
        scratch_shapes=[
            pltpu.VMEM((N_HEADS, 2 * dl, HEAD_W), BF16),
            pltpu.VMEM((N_HEADS, 2 * dl, 1), F32),
            pltpu.VMEM((N_HEADS, 2 * dl, 1), F32),
            pltpu.VMEM((N_HEADS, 2 * dl, HEAD_W), F32),
        ],
    )
    kern = functools.partial(_paged_kernel, n_pg=n_pg, lam_init=lam_init)
    return pl.pallas_call(
        kern,
        grid_spec=grid_spec,
        out_shape=jax.ShapeDtypeStruct((DB, N_HEADS, dl, HEAD_W), F32),
        compiler_params=_cparams(("arbitrary", "arbitrary")),
        name="paged_diff_attention",
    )(page_table.reshape(-1), lam_p, g_subln, q_heads, k_new, v_new,
      *([cache_k] * n_pg), *([cache_v] * n_pg))


def _seg_scan(x, pos, seg, op, ident):
    s = 1
    while s < seg:
        shifted = pltpu.roll(x, s, 0)
        x = op(x, jnp.where(pos >= s, shifted, ident))
        s *= 2
    return x


def _seg_last(x, n_seg, seg):
    r, w = x.shape
    x3 = x.reshape(n_seg, seg, w)
    return jnp.broadcast_to(x3[:, seg - 1:seg, :], (n_seg, seg, w)).reshape(r, w)


def _mlstm_kernel(*refs, n_seg, seg, has_state):
    if has_state:
        (qk_ref, v_ref, o_ref, gt_ref, bg_ref, gn_ref, c_in, n_in, m_in,
         h_ref, c_out, n_out, m_out) = refs
    else:
        qk_ref, v_ref, o_ref, gt_ref, bg_ref, gn_ref, h_ref, c_out, n_out, m_out = refs
        c_in, n_in, m_in = c_out, n_out, m_out

        @pl.when(pl.program_id(1) == 0)
        def _():
            c_out[...] = jnp.zeros(c_out.shape, F32)
            n_out[...] = jnp.zeros(n_out.shape, F32)
            m_out[...] = jnp.zeros(m_out.shape, F32)

    R = n_seg * seg
    D = qk_ref.shape[1]
    KOFF = D // 2
    QK = ML_QK_DIM

    pre = gt_ref[...] + bg_ref[...]
    pre = GATE_CAP * jnp.tanh(pre * (1.0 / GATE_CAP))
    li = pre[:, :LANES]
    xf = pre[:, LANES:]
    lf = jnp.minimum(xf, 0.0) - jnp.log1p(jnp.exp(-jnp.abs(xf)))

    rowi = lax.broadcasted_iota(jnp.int32, (R, LANES), 0)
    pos = rowi & (seg - 1)
    bcum = _seg_scan(lf, pos, seg, jnp.add, 0.0)
    a = li - bcum
    cm = _seg_scan(a, pos, seg, jnp.maximum, NEG)
    m0 = m_in[...]
    mx = jnp.maximum(m0, cm)
    mt = bcum + mx
    u = -mx
    w_inter = jnp.exp(m0 + u)
    emt = jnp.exp(-mt)
    b_last = _seg_last(bcum, n_seg, seg)
    m_new = _seg_last(mt, n_seg, seg)
    decay = jnp.exp(b_last + m0 - m_new)
    w_state = jnp.exp(a + b_last - m_new)
    a_t = a.T

    tr = lax.broadcasted_iota(jnp.int32, (R, R), 0)
    tc = lax.broadcasted_iota(jnp.int32, (R, R), 1)
    if n_seg == 1:
        intra = tc <= tr
    else:
        shift = seg.bit_length() - 1
        intra = ((tr >> shift) == (tc >> shift)) & (tc <= tr)

    lane = lax.broadcasted_iota(jnp.int32, (R, LANES), 1)
    low = lane < QK
    hr = lax.broadcasted_iota(jnp.int32, (LANES, LANES), 0)
    hc = lax.broadcasted_iota(jnp.int32, (LANES, LANES), 1)
    half_sel = jnp.where(hc == jnp.where(hr < QK, 0, 1), 1.0, 0.0).astype(BF16)
    low_row = low[0:1]
    if n_seg > 1:
        colseg = lax.broadcasted_iota(jnp.int32, (QK, R), 1) >> (seg.bit_length() - 1)

    for p in range(N_HEADS // 2):
        ha, hb = 2 * p, 2 * p + 1
        qp = qk_ref[:, p * LANES:(p + 1) * LANES].astype(F32)
        kp_bf = qk_ref[:, KOFF + p * LANES:KOFF + (p + 1) * LANES]
        q_heads = (jnp.where(low, qp, 0.0).astype(BF16), jnp.where(low, 0.0, qp).astype(BF16))
        kw = kp_bf.astype(F32) * jnp.where(low, w_state[:, ha:ha + 1], w_state[:, hb:hb + 1])
        kw_t = kw.T
        if n_seg == 1:
            n_rows = n_in[0, p:p + 1, :]
        else:
            n_rows = jnp.concatenate(
                [jnp.broadcast_to(n_in[g, p:p + 1, :], (seg, LANES)) for g in range(n_seg)], axis=0)
        qn = jnp.dot((qp * n_rows).astype(BF16), half_sel, preferred_element_type=F32)
        sp = [c_in[g, ha:hb + 1].reshape(2 * QK, LANES).astype(BF16) for g in range(n_seg)]
        for hi, h in enumerate((ha, hb)):
            qh = q_heads[hi]
            s = lax.dot_general(qh, kp_bf, _NT, preferred_element_type=F32)
            wi = jnp.where(intra, jnp.exp(u[:, h:h + 1] + a_t[h:h + 1, :]), 0.0)
            sw = wi * s
            vh = v_ref[:, h * LANES:(h + 1) * LANES]
            num = jnp.dot(sw.astype(BF16), vh, preferred_element_type=F32)
            den = jnp.sum(sw, axis=-1, keepdims=True)
            if n_seg == 1:
                inter = jnp.dot(qh, sp[0], preferred_element_type=F32)
            else:
                inter = jnp.concatenate(
                    [jnp.dot(qh[g * seg:(g + 1) * seg], sp[g], preferred_element_type=F32)
                     for g in range(n_seg)], axis=0)
            wcol = w_inter[:, h:h + 1]
            num = num + inter * wcol
            den = den + wcol * qn[:, hi:hi + 1]
            denom = jnp.maximum(jnp.abs(den), emt[:, h:h + 1])
            hh = num / denom
            hn = _rmsnorm_rows(hh, gn_ref[:, h * LANES:(h + 1) * LANES])
            og = jax.nn.sigmoid(o_ref[:, h * LANES:(h + 1) * LANES])
            h_ref[:, h * LANES:(h + 1) * LANES] = (og * hn).astype(BF16)
            kw_h = kw_t[hi * QK:(hi + 1) * QK, :]
            if n_seg == 1:
                upd = jnp.dot(kw_h.astype(BF16), vh, preferred_element_type=F32)
                c_out[0, h] = decay[0:1, h:h + 1] * c_in[0, h] + upd
            else:
                lhs = jnp.concatenate(
                    [jnp.where(colseg == g, kw_h, 0.0) for g in range(n_seg)], axis=0).astype(BF16)
                upd = jnp.dot(lhs, vh, preferred_element_type=F32)
                for g in range(n_seg):
                    c_out[g, h] = decay[g * seg:g * seg + 1, h:h + 1] * c_in[g, h] + upd[g * QK:(g + 1) * QK]
        for g in range(n_seg):
            r0 = g * seg
            ksum = jnp.sum(kw[r0:r0 + seg], axis=0, keepdims=True)
            dpair = jnp.where(low_row, decay[r0:r0 + 1, ha:ha + 1], decay[r0:r0 + 1, hb:hb + 1])
            n_out[g, p:p + 1, :] = dpair * n_in[g, p:p + 1, :] + ksum
    m_out[...] = m_new


def mlstm_prompt(qk, v, o, gates, bgate, gnorm, batch, seq):
    T, D = qk.shape
    R = LANES
    nc = seq // R
    GW = gates.shape[1]
    row = lambda b, c: (b * nc + c, 0)
    const = lambda b, c: (0, 0)
    st = lambda b, c: (b, 0, 0, 0)
    kern = functools.partial(_mlstm_kernel, n_seg=1, seg=R, has_state=False)
    return pl.pallas_call(
        kern,
        grid=(batch, nc),
        in_specs=[
            pl.BlockSpec((R, D), row), pl.BlockSpec((R, D), row), pl.BlockSpec((R, D), row),
            pl.BlockSpec((R, GW), row), pl.BlockSpec((1, GW), const), pl.BlockSpec((1, D), const),
        ],
        out_specs=[
            pl.BlockSpec((R, D), row),
            pl.BlockSpec((1, N_HEADS, ML_QK_DIM, LANES), st),
            pl.BlockSpec((1, N_HEADS // 2, LANES), lambda b, c: (b, 0, 0)),
            pl.BlockSpec((R, LANES), lambda b, c: (b, 0)),
        ],
        out_shape=(jax.ShapeDtypeStruct((T, D), BF16),
                   jax.ShapeDtypeStruct((batch, N_HEADS, ML_QK_DIM, LANES), F32),
                   jax.ShapeDtypeStruct((batch, N_HEADS // 2, LANES), F32),
                   jax.ShapeDtypeStruct((batch * R, LANES), F32)),
        compiler_params=_cparams(("arbitrary", "arbitrary")),
        name="mlstm_prompt",
    )(qk, v, o, gates, bgate, gnorm)


def mlstm_sample(qk, v, o, gates, bgate, gnorm, c0, n0_pairs, m0_rows, seg):
    T, D = qk.shape
    R = LANES
    n_seg = R // seg
    GW = gates.shape[1]
    row = lambda i: (i, 0)
    const = lambda i: (0, 0)
    st = lambda i: (i, 0, 0, 0)
    st_block = (n_seg, N_HEADS, ML_QK_DIM, LANES)
    n_block = (n_seg, N_HEADS // 2, LANES)
    nst = lambda i: (i, 0, 0)
    kern = functools.partial(_mlstm_kernel, n_seg=n_seg, seg=seg, has_state=True)
    return pl.pallas_call(
        kern,
        grid=(T // R,),
        in_specs=[
            pl.BlockSpec((R, D), row), pl.BlockSpec((R, D), row), pl.BlockSpec((R, D), row),
            pl.BlockSpec((R, GW), row), pl.BlockSpec((1, GW), const), pl.BlockSpec((1, D), const),
            pl.BlockSpec(st_block, st), pl.BlockSpec(n_block, nst), pl.BlockSpec((R, LANES), row),
        ],
        out_specs=[
            pl.BlockSpec((R, D), row),
            pl.BlockSpec(st_block, st),
            pl.BlockSpec(n_block, nst),
            pl.BlockSpec((R, LANES), row),
        ],
        out_shape=(jax.ShapeDtypeStruct((T, D), BF16),
                   jax.ShapeDtypeStruct(c0.shape, F32),
                   jax.ShapeDtypeStruct(n0_pairs.shape, F32),
                   jax.ShapeDtypeStruct((T, LANES), F32)),
        compiler_params=_cparams(("arbitrary",)),
        name="mlstm_sample",
    )(qk, v, o, gates, bgate, gnorm, c0, n0_pairs, m0_rows)


def _ffn_core(x_ref, a_ref, wo_ref, gn_ref, wg_ref, wu_ref, wd_ref, cw_ref, y_ref, hn_ref, shifted,
              final_g_ref):
    f = pl.program_id(1)

    @pl.when(f == 0)
    def _():
        xm = x_ref[...] + jnp.dot(a_ref[...], wo_ref[...], preferred_element_type=F32)
        y_ref[...] = xm
        hn_ref[...] = _rmsnorm_rows(xm, gn_ref[...]).astype(BF16)

    hn = hn_ref[...]
    g = jnp.dot(hn, wg_ref[...], preferred_element_type=F32)
    u = jnp.dot(hn, wu_ref[...], preferred_element_type=F32)
    g1, g2 = shifted(g)
    cw = cw_ref[...]
    gc = cw[3:4] + cw[0:1] * g2 + cw[1:2] * g1 + cw[2:3] * g
    act = (jax.nn.silu(gc) * u).astype(BF16)
    y_ref[...] += jnp.dot(act, wd_ref[...], preferred_element_type=F32)

    if final_g_ref is not None:
        @pl.when(f == pl.num_programs(1) - 1)
        def _():
            y_ref[...] = _rmsnorm_rows(y_ref[...], final_g_ref[...])
    return g


def _ffn_prompt_kernel(*refs, tiles_per_seq, final_norm):
    if final_norm:
        (x_ref, a_ref, wo_ref, gn_ref, wg_ref, wu_ref, wd_ref, cw_ref, fg_ref,
         y_ref, st_ref, hn_ref, gbuf, carry) = refs
    else:
        x_ref, a_ref, wo_ref, gn_ref, wg_ref, wu_ref, wd_ref, cw_ref, y_ref, st_ref, hn_ref, gbuf, carry = refs
        fg_ref = None
    i = pl.program_id(0)
    f = pl.program_id(1)
    tm = x_ref.shape[0]

    @pl.when((i == 0) & (f == 0))
    def _():
        carry[...] = jnp.zeros(carry.shape, F32)

    def shifted(g):
        prev = jnp.where(i % tiles_per_seq == 0, 0.0, carry[f])
        gbuf[0:SUBLANES] = prev
        gbuf[SUBLANES:] = g
        return gbuf[SUBLANES - 1:SUBLANES - 1 + tm], gbuf[SUBLANES - 2:SUBLANES - 2 + tm]

    g = _ffn_core(x_ref, a_ref, wo_ref, gn_ref, wg_ref, wu_ref, wd_ref, cw_ref, y_ref, hn_ref, shifted, fg_ref)
    tail = g[tm - SUBLANES:]
    carry[f] = tail
    st_ref[0] = tail


def _ffn_sample_kernel(*refs, seg, final_norm):
    if final_norm:
        x_ref, a_ref, wo_ref, gn_ref, wg_ref, wu_ref, wd_ref, cw_ref, e_ref, fg_ref, y_ref, g_ref, hn_ref = refs
    else:
        x_ref, a_ref, wo_ref, gn_ref, wg_ref, wu_ref, wd_ref, cw_ref, e_ref, y_ref, g_ref, hn_ref = refs
        fg_ref = None
    tm = x_ref.shape[0]

    def shifted(g):
        e = e_ref[...]
        pos = lax.broadcasted_iota(jnp.int32, g.shape, 0) & (seg - 1)
        g1 = jnp.where(pos < 1, pltpu.roll(e, tm - 1, 0), pltpu.roll(g, 1, 0))
        g2 = jnp.where(pos < 2, e, pltpu.roll(g, 2, 0))
        return g1, g2

    g = _ffn_core(x_ref, a_ref, wo_ref, gn_ref, wg_ref, wu_ref, wd_ref, cw_ref, y_ref, hn_ref, shifted, fg_ref)
    g_ref[...] = g


def _ffn_common_specs(D, tm, tf, n_f):
    return [
        pl.BlockSpec((tm, D), lambda i, f: (i, 0)),
        pl.BlockSpec((tm, D), lambda i, f: (i, 0)),
        pl.BlockSpec((D, D), lambda i, f: (0, 0)),
        pl.BlockSpec((1, D), lambda i, f: (0, 0)),
        pl.BlockSpec((D, tf), lambda i, f: (0, f)),
        pl.BlockSpec((D, tf), lambda i, f: (0, n_f + f)),
        pl.BlockSpec((tf, D), lambda i, f: (f, 0)),
        pl.BlockSpec((SUBLANES, tf), lambda i, f: (0, f)),
    ]


def ffn_prompt(x, a_bf, w_o_bf, gn, w_up_bf, w_down_bf, cw, batch, seq, tm, tf, final_g=None):
    T, D = x.shape
    F = w_down_bf.shape[0]
    n_f = F // tf
    tps = seq // tm
    in_specs = _ffn_common_specs(D, tm, tf, n_f)
    args = [x, a_bf, w_o_bf, gn, w_up_bf, w_up_bf, w_down_bf, cw]
    if final_g is not None:
        in_specs.append(pl.BlockSpec((1, D), lambda i, f: (0, 0)))
        args.append(final_g)
    kern = functools.partial(_ffn_prompt_kernel, tiles_per_seq=tps, final_norm=final_g is not None)
    return pl.pallas_call(
        kern,
        grid=(T // tm, n_f),
        in_specs=in_specs,
        out_specs=[
            pl.BlockSpec((tm, D), lambda i, f: (i, 0)),
            pl.BlockSpec((1, SUBLANES, tf), lambda i, f: (i, 0, f)),
        ],
        out_shape=(jax.ShapeDtypeStruct((T, D), F32),
                   jax.ShapeDtypeStruct((T // tm, SUBLANES, F), F32)),
        scratch_shapes=[
            pltpu.VMEM((tm, D), BF16),
            pltpu.VMEM((tm + SUBLANES, tf), F32),
            pltpu.VMEM((n_f, SUBLANES, tf), F32),
        ],
        compiler_params=_cparams(("arbitrary", "arbitrary")),
        name="ffn_prompt",
    )(*args)


def ffn_sample(x, a_bf, w_o_bf, gn, w_up_bf, w_down_bf, cw, e_rows, seg, tm, tf, final_g=None):
    T, D = x.shape
    F = w_down_bf.shape[0]
    n_f = F // tf
    in_specs = _ffn_common_specs(D, tm, tf, n_f)
    in_specs.append(pl.BlockSpec((tm, tf), lambda i, f: (i, f)))
    args = [x, a_bf, w_o_bf, gn, w_up_bf, w_up_bf, w_down_bf, cw, e_rows]
    if final_g is not None:
        in_specs.append(pl.BlockSpec((1, D), lambda i, f: (0, 0)))
        args.append(final_g)
    kern = functools.partial(_ffn_sample_kernel, seg=seg, final_norm=final_g is not None)
    return pl.pallas_call(
        kern,
        grid=(T // tm, n_f),
        in_specs=in_specs,
        out_specs=[
            pl.BlockSpec((tm, D), lambda i, f: (i, 0)),
            pl.BlockSpec((tm, tf), lambda i, f: (i, f)),
        ],
        out_shape=(jax.ShapeDtypeStruct((T, D), F32),
                   jax.ShapeDtypeStruct((T, F), F32)),
        scratch_shapes=[pltpu.VMEM((tm, D), BF16)],
        compiler_params=_cparams(("arbitrary", "arbitrary")),
        name="ffn_sample",
    )(*args)


def kernel(x_prompt, x_sample, cache_k, cache_v, page_table, state_mlstm_c, state_mlstm_n, state_mlstm_m,
           state_conv, g_mix_norm, g_ffn_norm, g_final, attn_w_qkv, attn_lambda, attn_subln, attn_w_o,
           ml_w_in, ml_b_gate, ml_norm, ml_w_out, ffn_w_up, ffn_conv_w, ffn_conv_b, ffn_w_down):
    B, L, D = x_prompt.shape
    DB, DL, _ = x_sample.shape
    depth = g_mix_norm.shape[0]
    n_attn, n_pool = cache_k.shape[:2]
    n_pages = page_table.shape[1]
    past = n_pages * PAGE_SIZE
    d_ff = ffn_w_down.shape[1]
    n_ml_heads = state_mlstm_c.shape[2]
    assert D == N_HEADS * HEAD_W and n_ml_heads == N_HEADS and DL == SUBLANES
    assert cache_k.shape[2:] == (PAGE_SIZE, N_HEADS, HEAD_W) and cache_v.shape[2:] == (PAGE_SIZE, N_HEADS, HEAD_W)

    TP, TS = B * L, DB * DL
    TM = 512
    TM_S = min(TM, TS)
    TM_FFN = 1024
    TM_FFN_S = min(TM_FFN, TS)
    TF = 256
    TQ = 512
    N_PG = 8

    xp = x_prompt.reshape(TP, D)
    xs = x_sample.reshape(TS, D)
    rope_p, rope_s = rope_tables(L, past, DL, TM_S)
    ck = cache_k.reshape(n_attn * n_pool, PAGE_SIZE * N_HEADS, HEAD_W)
    cv = cache_v.reshape(n_attn * n_pool, PAGE_SIZE * N_HEADS, HEAD_W)

    qk_w = N_HEADS * ML_QK_DIM
    k_p, v_p, k_s, v_s = [], [], [], []
    c_p, n_p, m_p, c_s, n_s, m_s = [], [], [], [], [], []
    cv_p, cv_s = [], []
    for i in range(depth):
        j = i // 2
        g_mix = g_mix_norm[i].reshape(1, D)
        if i % 2 == 0:
            lam_init = 0.8 - 0.6 * math.exp(-0.3 * i)
            w_qkv = attn_w_qkv[j].astype(BF16)
            w_o = attn_w_o[j].astype(BF16)
            g_sub = attn_subln[j].reshape(1, HEAD_W)
            lam_p = attn_lambda[j]
            qp, kbp, vbp, kp, vp = qkv_proj(xp, g_mix, w_qkv, rope_p, TM, L // TM)
            qs, _, _, ks, vs = qkv_proj(xs, g_mix, w_qkv, rope_s, TM_S, 1)
            ap = flash_diff_attention(qp, kbp, vbp, lam_p, g_sub, B, L, lam_init, TQ)
            rows = DL * N_HEADS
            q_heads = jnp.transpose(qs.reshape(DB, DL, N_HEADS, HEAD_W), (0, 2, 1, 3)).astype(F32)
            as_ = paged_diff_attention(
                q_heads, ks.reshape(DB, rows, HEAD_W), vs.reshape(DB, rows, HEAD_W),
                ck, cv, page_table, j * n_pool, lam_p, g_sub, lam_init, N_PG)
            as_ = jnp.transpose(as_, (0, 2, 1, 3)).reshape(TS, D).astype(BF16)
            mix_p, mix_s, w_mix = ap, as_, w_o
            k_p.append(kp.reshape(B, L, N_HEADS, HEAD_W))
            v_p.append(vp.reshape(B, L, N_HEADS, HEAD_W))
            k_s.append(ks.reshape(DB, DL, N_HEADS, HEAD_W))
            v_s.append(vs.reshape(DB, DL, N_HEADS, HEAD_W))
        else:
            w_in = ml_w_in[j]
            w_main = w_in[:, :2 * qk_w + 2 * D].astype(BF16)
            w_gate = w_in[:, 2 * qk_w + 2 * D:]
            pad = jnp.zeros((D, LANES - N_HEADS), F32)
            wg = jnp.concatenate([w_gate[:, :N_HEADS], pad, w_gate[:, N_HEADS:], pad], axis=1).astype(BF16)
            bg = ml_b_gate[j]
            padb = jnp.zeros((LANES - N_HEADS,), F32)
            bgate = jnp.concatenate([bg[:N_HEADS], padb, bg[N_HEADS:], padb]).reshape(1, 2 * LANES)
            gnorm = ml_norm[j].reshape(1, D)
            w_out = ml_w_out[j].astype(BF16)
            qkp, vvp, oop, gtp = mlstm_in_proj(xp, g_mix, w_main, wg, TM)
            qks, vvs, oos, gts = mlstm_in_proj(xs, g_mix, w_main, wg, TM_S)
            hp, cp_, np_, mp_ = mlstm_prompt(qkp, vvp, oop, gtp, bgate, gnorm, B, L)
            n0_pairs = state_mlstm_n[j].reshape(DB, N_HEADS // 2, LANES)
            m0_rows = jnp.repeat(jnp.pad(state_mlstm_m[j], ((0, 0), (0, LANES - N_HEADS))), DL, axis=0)
            hs, cs_, ns_, ms_ = mlstm_sample(qks, vvs, oos, gts, bgate, gnorm,
                                             state_mlstm_c[j], n0_pairs, m0_rows, DL)
            mix_p, mix_s, w_mix = hp, hs, w_out
            c_p.append(cp_)
            n_p.append(np_.reshape(B, N_HEADS, ML_QK_DIM))
            m_p.append(mp_.reshape(B, LANES, LANES)[:, 0, :N_HEADS])
            c_s.append(cs_)
            n_s.append(ns_.reshape(DB, N_HEADS, ML_QK_DIM))
            m_s.append(ms_.reshape(DB, DL, LANES)[:, DL - 1, :N_HEADS])
        g_ffn = g_ffn_norm[i].reshape(1, D)
        w_up = ffn_w_up[i].astype(BF16)
        w_down = ffn_w_down[i].astype(BF16)
        cw = jnp.concatenate([ffn_conv_w[i], ffn_conv_b[i][None, :],
                              jnp.zeros((SUBLANES - CONV_W - 1, d_ff), F32)], axis=0)
        fg = g_final.reshape(1, D) if i == depth - 1 else None
        xp, stp = ffn_prompt(xp, mix_p, w_mix, g_ffn, w_up, w_down, cw, B, L, TM_FFN, TF, fg)
        e_rows = jnp.pad(state_conv[i], ((0, 0), (0, DL - (CONV_W - 1)), (0, 0))).reshape(TS, d_ff)
        xs, gs = ffn_sample(xs, mix_s, w_mix, g_ffn, w_up, w_down, cw, e_rows, DL, TM_FFN_S, TF, fg)
        cv_p.append(stp.reshape(B, L // TM_FFN, SUBLANES, d_ff)[:, -1, SUBLANES - (CONV_W - 1):, :])
        cv_s.append(gs.reshape(DB, DL, d_ff)[:, DL - (CONV_W - 1):, :])

    return (xp.reshape(B, L, D), xs.reshape(DB, DL, D),
            jnp.stack(k_p), jnp.stack(v_p), jnp.stack(k_s), jnp.stack(v_s),
            jnp.stack(c_p), jnp.stack(n_p), jnp.stack(m_p),
            jnp.stack(c_s), jnp.stack(n_s), jnp.stack(m_s),
            jnp.stack(cv_p), jnp.stack(cv_s))
```

```python
import functools
import math

import jax
import jax.numpy as jnp
from jax import lax
from jax.experimental import pallas as pl
from jax.experimental.pallas import tpu as pltpu

F32 = jnp.float32
BF16 = jnp.bfloat16

LANES = 128
SUBLANES = 8
VMEM_LIMIT = 56 * 1024 * 1024

EPS = 1e-6
ROPE_THETA = 10000.0
PAGE_SIZE = 128
N_HEADS = 8
HEAD_W = 128
DA_HEAD_DIM = 64
ML_QK_DIM = 64
GATE_CAP = 15.0
CONV_W = 3
NEG = -1e30
LOG2E = math.log2(math.e)


def _cparams(sem):
    return pltpu.CompilerParams(dimension_semantics=sem, vmem_limit_bytes=VMEM_LIMIT)


def _rmsnorm_rows(x, g):
    return x * lax.rsqrt(jnp.mean(x * x, axis=-1, keepdims=True) + EPS) * g


def _rope_table_kernel(tp_ref, ts_ref, *, past, dec_seq):
    def tables(pos, shape):
        lane = lax.broadcasted_iota(jnp.int32, shape, 1)
        half = DA_HEAD_DIM // 2
        fidx = (lane & (half - 1)).astype(F32)
        freq = jnp.exp(fidx * (-math.log(ROPE_THETA) / half))
        ang = pos * freq
        c = jnp.cos(ang)
        s = jnp.sin(ang)
        first = (lane & (DA_HEAD_DIM - 1)) < half
        return c, jnp.where(first, -s, 0.0), jnp.where(first, 0.0, s)

    shp = tp_ref.shape[1:]
    pos_p = lax.broadcasted_iota(jnp.int32, shp, 0).astype(F32)
    c, s1, s2 = tables(pos_p, shp)
    tp_ref[0] = c
    tp_ref[1] = s1
    tp_ref[2] = s2
    shs = ts_ref.shape[1:]
    row = lax.broadcasted_iota(jnp.int32, shs, 0)
    pos_s = (past + (row & (dec_seq - 1))).astype(F32)
    c, s1, s2 = tables(pos_s, shs)
    ts_ref[0] = c
    ts_ref[1] = s1
    ts_ref[2] = s2


def rope_tables(seq, past, dec_seq, sample_rows):
    return pl.pallas_call(
        functools.partial(_rope_table_kernel, past=past, dec_seq=dec_seq),
        out_shape=(jax.ShapeDtypeStruct((3, seq, LANES), F32),
                   jax.ShapeDtypeStruct((3, sample_rows, LANES), F32)),
        name="rope_tables",
    )()


def _norm_into(hn_ref, x_ref, g_ref):
    hn_ref[...] = _rmsnorm_rows(x_ref[...], g_ref[...]).astype(BF16)


def _rope_head(y, c, s1, s2):
    return y * c + pltpu.roll(y, LANES - 32, 1) * s1 + pltpu.roll(y, 32, 1) * s2


def _qkv_kernel(x_ref, g_ref, w_ref, rope_ref, q_ref, kb_ref, vb_ref, k3_ref, v3_ref, hn_ref, tr_ref):
    j = pl.program_id(1)
    tm = x_ref.shape[0]

    @pl.when(j == 0)
    def _():
        _norm_into(hn_ref, x_ref, g_ref)

    y = jnp.dot(hn_ref[...], w_ref[...], preferred_element_type=F32)

    @pl.when(j == 0)
    def _():
        c, s1, s2 = rope_ref[0], rope_ref[1], rope_ref[2]
        for h in range(N_HEADS):
            sl = slice(h * HEAD_W, (h + 1) * HEAD_W)
            q_ref[:, sl] = (_rope_head(y[:, sl], c, s1, s2) * (DA_HEAD_DIM ** -0.5 * LOG2E)).astype(BF16)

    def head_tile(h):
        return slice(h * SUBLANES, (h + 1) * SUBLANES)

    def to_cache_rows(out_ref):
        for s in range(SUBLANES):
            out_ref[:, s] = tr_ref[:, pl.ds(s, N_HEADS, stride=SUBLANES), :]

    @pl.when(j == 1)
    def _():
        c, s1, s2 = rope_ref[0], rope_ref[1], rope_ref[2]
        for h in range(N_HEADS):
            sl = slice(h * HEAD_W, (h + 1) * HEAD_W)
            kh = _rope_head(y[:, sl], c, s1, s2)
            kb_ref[:, sl] = kh.astype(BF16)
            tr_ref[:, head_tile(h), :] = kh.reshape(tm // SUBLANES, SUBLANES, HEAD_W)
        to_cache_rows(k3_ref)

    @pl.when(j == 2)
    def _():
        vb_ref[...] = y.astype(BF16)
        for h in range(N_HEADS):
            tr_ref[:, head_tile(h), :] = y[:, h * HEAD_W:(h + 1) * HEAD_W].reshape(
                tm // SUBLANES, SUBLANES, HEAD_W)
        to_cache_rows(v3_ref)


def qkv_proj(x, g, w_bf, rope_tab, tm, rope_blocks):
    T, D = x.shape
    n_i = T // tm

    def rope_idx(i, j):
        return (0, i % rope_blocks, 0)

    flat = pl.BlockSpec((tm, D), lambda i, j: (i, 0))
    tg = tm // SUBLANES
    rows3 = pl.BlockSpec((tg, SUBLANES, N_HEADS, HEAD_W), lambda i, j: (i, 0, 0, 0))
    return pl.pallas_call(
        _qkv_kernel,
        grid=(n_i, 3),
        in_specs=[
            flat,
            pl.BlockSpec((1, D), lambda i, j: (0, 0)),
            pl.BlockSpec((D, D), lambda i, j: (0, j)),
            pl.BlockSpec((3, tm, LANES), rope_idx),
        ],
        out_specs=[flat, flat, flat, rows3, rows3],
        out_shape=(jax.ShapeDtypeStruct((T, D), BF16),
                   jax.ShapeDtypeStruct((T, D), BF16),
                   jax.ShapeDtypeStruct((T, D), BF16),
                   jax.ShapeDtypeStruct((T // SUBLANES, SUBLANES, N_HEADS, HEAD_W), F32),
                   jax.ShapeDtypeStruct((T // SUBLANES, SUBLANES, N_HEADS, HEAD_W), F32)),
        scratch_shapes=[pltpu.VMEM((tm, D), BF16),
                        pltpu.VMEM((tg, N_HEADS * SUBLANES, HEAD_W), F32)],
        compiler_params=_cparams(("arbitrary", "arbitrary")),
        name="qkv_proj",
    )(x, g, w_bf, rope_tab)


def _mlin_kernel(x_ref, g_ref, w_ref, wg_ref, qk_ref, v_ref, o_ref, gt_ref, hn_ref):
    j = pl.program_id(1)

    @pl.when(j == 0)
    def _():
        _norm_into(hn_ref, x_ref, g_ref)
        gt_ref[...] = jnp.dot(hn_ref[...], wg_ref[...], preferred_element_type=F32)

    y = jnp.dot(hn_ref[...], w_ref[...], preferred_element_type=F32)

    @pl.when(j == 0)
    def _():
        half = y.shape[1] // 2
        qk_ref[:, :half] = (y[:, :half] * (ML_QK_DIM ** -0.5)).astype(BF16)
        qk_ref[:, half:] = y[:, half:].astype(BF16)

    @pl.when(j == 1)
    def _():
        v_ref[...] = y.astype(BF16)

    @pl.when(j == 2)
    def _():
        o_ref[...] = y


def mlstm_in_proj(x, g, w_bf, wg_bf, tm):
    T, D = x.shape
    GW = wg_bf.shape[1]
    return pl.pallas_call(
        _mlin_kernel,
        grid=(T // tm, 3),
        in_specs=[
            pl.BlockSpec((tm, D), lambda i, j: (i, 0)),
            pl.BlockSpec((1, D), lambda i, j: (0, 0)),
            pl.BlockSpec((D, D), lambda i, j: (0, j)),
            pl.BlockSpec((D, GW), lambda i, j: (0, 0)),
        ],
        out_specs=[
            pl.BlockSpec((tm, D), lambda i, j: (i, 0)),
            pl.BlockSpec((tm, D), lambda i, j: (i, 0)),
            pl.BlockSpec((tm, D), lambda i, j: (i, 0)),
            pl.BlockSpec((tm, GW), lambda i, j: (i, 0)),
        ],
        out_shape=(jax.ShapeDtypeStruct((T, D), BF16),
                   jax.ShapeDtypeStruct((T, D), BF16),
                   jax.ShapeDtypeStruct((T, D), F32),
                   jax.ShapeDtypeStruct((T, GW), F32)),
        scratch_shapes=[pltpu.VMEM((tm, D), BF16)],
        compiler_params=_cparams(("arbitrary", "arbitrary")),
        name="mlstm_in_proj",
    )(x, g, w_bf, wg_bf)


def _diff_lambda(lam_ref, lam_init):
    lp = lam_ref[...]
    a = jnp.sum(lp[0:1] * lp[1:2], axis=-1, keepdims=True)
    b = jnp.sum(lp[2:3] * lp[3:4], axis=-1, keepdims=True)
    return jnp.exp(a) - jnp.exp(b) + lam_init


def _split_maps(q):
    qf = q.astype(F32)
    lane = lax.broadcasted_iota(jnp.int32, qf.shape, 1)
    return (jnp.where(lane < DA_HEAD_DIM, qf, 0.0).astype(BF16),
            jnp.where(lane >= DA_HEAD_DIM, qf, 0.0).astype(BF16))


def _softmax_update(s, v_bf, m_ref, l_ref, acc_ref, idx):
    m_old = m_ref[idx]
    m_new = jnp.maximum(m_old, jnp.max(s, axis=-1, keepdims=True))
    alpha = jnp.exp2(m_old - m_new)
    p = jnp.exp2(s - m_new)
    l_ref[idx] = alpha * l_ref[idx] + jnp.sum(p, axis=-1, keepdims=True)
    acc_ref[idx] = alpha * acc_ref[idx] + jnp.dot(p.astype(BF16), v_bf, preferred_element_type=F32)
    m_ref[idx] = m_new


def _diff_finish(o1, o2, lam, g, lam_init):
    o = o1 - lam * o2
    return (_rmsnorm_rows(o, g) * (1.0 - lam_init)).astype(BF16)


_NT = (((1,), (1,)), ((), ()))


def _flash_kernel(lam_ref, g_ref, q_ref, k_ref, v_ref, o_ref, vt, m_s, l_s, acc_s, *, tq, lam_init):
    qi = pl.program_id(2)
    n_blk = vt.shape[0]

    @pl.when(qi == 0)
    def _():
        for c in range(n_blk):
            for cc in range(tq // LANES):
                r0 = c * tq + cc * LANES
                vt[c, :, cc * LANES:(cc + 1) * LANES] = v_ref[r0:r0 + LANES, :].astype(F32).T.astype(BF16)

    qm = _split_maps(q_ref[...])
    m_s[...] = jnp.full(m_s.shape, NEG, F32)
    l_s[...] = jnp.zeros(l_s.shape, F32)
    acc_s[...] = jnp.zeros(acc_s.shape, F32)

    def step(ki, masked):
        off = pl.multiple_of(ki * tq, tq)
        kblk = k_ref[pl.ds(off, tq), :]
        vblk = vt[ki]
        scores = [lax.dot_general(kblk, qm[mp], _NT, preferred_element_type=F32) for mp in range(2)]
        for mp in range(2):
            s = scores[mp]
            if masked:
                key = lax.broadcasted_iota(jnp.int32, s.shape, 0)
                qry = lax.broadcasted_iota(jnp.int32, s.shape, 1)
                s = jnp.where(key <= qry, s, NEG)
            m_old = m_s[mp]
            m_new = jnp.maximum(m_old, jnp.max(s, axis=0, keepdims=True))
            alpha = jnp.exp2(m_old - m_new)
            p = jnp.exp2(s - m_new)
            l_s[mp] = alpha * l_s[mp] + jnp.sum(p, axis=0, keepdims=True)
            acc_s[mp] = alpha * acc_s[mp] + jnp.dot(vblk, p.astype(BF16), preferred_element_type=F32)
            m_s[mp] = m_new

    def body(ki, carry):
        step(ki, False)
        return carry

    lax.fori_loop(0, qi, body, 0)
    step(qi, True)

    lam = _diff_lambda(lam_ref, lam_init)
    o_t = acc_s[0] / l_s[0] - lam * (acc_s[1] / l_s[1])
    o_ref[...] = (_rmsnorm_rows(o_t.T, g_ref[...]) * (1.0 - lam_init)).astype(BF16)


def flash_diff_attention(q_bf, k_f32, v_f32, lam_p, g_subln, batch, seq, lam_init, tq):
    T, D = q_bf.shape
    nq = seq // tq
    kern = functools.partial(_flash_kernel, tq=tq, lam_init=lam_init)
    return pl.pallas_call(
        kern,
        grid=(batch, N_HEADS, nq),
        in_specs=[
            pl.BlockSpec(lam_p.shape, lambda b, h, i: (0, 0)),
            pl.BlockSpec((1, HEAD_W), lambda b, h, i: (0, 0)),
            pl.BlockSpec((tq, HEAD_W), lambda b, h, i: (b * nq + i, h)),
            pl.BlockSpec((seq, HEAD_W), lambda b, h, i: (b, h)),
            pl.BlockSpec((seq, HEAD_W), lambda b, h, i: (b, h)),
        ],
        out_specs=pl.BlockSpec((tq, HEAD_W), lambda b, h, i: (b * nq + i, h)),
        out_shape=jax.ShapeDtypeStruct((T, D), BF16),
        scratch_shapes=[
            pltpu.VMEM((nq, HEAD_W, tq), BF16),
            pltpu.VMEM((2, 1, tq), F32),
            pltpu.VMEM((2, 1, tq), F32),
            pltpu.VMEM((2, HEAD_W, tq), F32),
        ],
        compiler_params=_cparams(("arbitrary", "arbitrary", "arbitrary")),
        name="flash_diff_attention",
    )(lam_p, g_subln, q_bf, k_f32, v_f32)


def _paged_kernel(pt_ref, lam_ref, g_ref, q_ref, kn_ref, vn_ref, *rest, n_pg, lam_init):
    del pt_ref
    k_refs = rest[:n_pg]
    v_refs = rest[n_pg:2 * n_pg]
    o_ref = rest[2 * n_pg]
    qbd, m_s, l_s, acc_s = rest[2 * n_pg + 1:]
    pg = pl.program_id(1)
    dl = q_ref.shape[2]
    page = k_refs[0].shape[1] // N_HEADS

    def head_rows(ref, h, n_tok):
        return ref[0, pl.ds(h, n_tok, stride=N_HEADS), :]

    @pl.when(pg == 0)
    def _():
        for h in range(N_HEADS):
            q = q_ref[0, h]
            lane = lax.broadcasted_iota(jnp.int32, q.shape, 1)
            qbd[h, 0:dl] = jnp.where(lane < DA_HEAD_DIM, q, 0.0).astype(BF16)
            qbd[h, dl:2 * dl] = jnp.where(lane >= DA_HEAD_DIM, q, 0.0).astype(BF16)
        m_s[...] = jnp.full(m_s.shape, NEG, F32)
        l_s[...] = jnp.zeros(l_s.shape, F32)
        acc_s[...] = jnp.zeros(acc_s.shape, F32)

    def all_heads(refs):
        return jnp.stack(
            [jnp.concatenate([head_rows(refs[r], h, page) for r in range(n_pg)], axis=0)
             for h in range(N_HEADS)], axis=0).astype(BF16)

    s = lax.dot_general(qbd[...], all_heads(k_refs), (((2,), (2,)), ((0,), (0,))),
                        preferred_element_type=F32)
    m_old = m_s[...]
    m_new = jnp.maximum(m_old, jnp.max(s, axis=-1, keepdims=True))
    alpha = jnp.exp2(m_old - m_new)
    p = jnp.exp2(s - m_new)
    l_s[...] = alpha * l_s[...] + jnp.sum(p, axis=-1, keepdims=True)
    pv = lax.dot_general(p.astype(BF16), all_heads(v_refs), (((2,), (1,)), ((0,), (0,))),
                         preferred_element_type=F32)
    acc_s[...] = alpha * acc_s[...] + pv
    m_s[...] = m_new

    @pl.when(pg == pl.num_programs(1) - 1)
    def _():
        lam = _diff_lambda(lam_ref, lam_init)
        for h in range(N_HEADS):
            kn = head_rows(kn_ref, h, dl).astype(BF16)
            vn = head_rows(vn_ref, h, dl).astype(BF16)
            s = lax.dot_general(qbd[h], kn, _NT, preferred_element_type=F32)
            q_tok = lax.broadcasted_iota(jnp.int32, s.shape, 0) & (dl - 1)
            k_tok = lax.broadcasted_iota(jnp.int32, s.shape, 1)
            s = jnp.where(k_tok <= q_tok, s, NEG)
            _softmax_update(s, vn, m_s, l_s, acc_s, h)
            o = acc_s[h] / l_s[h]
            o_ref[0, h] = _diff_finish(o[0:dl], o[dl:2 * dl], lam, g_ref[...], lam_init).astype(F32)


def paged_diff_attention(q_heads, k_new, v_new, cache_k, cache_v, page_table, pool_offset, lam_p, g_subln,
                         lam_init, n_pg):
    DB, _, dl, _ = q_heads.shape
    rows = k_new.shape[1]
    n_pages = page_table.shape[1]
    page_rows = cache_k.shape[1]
    steps = n_pages // n_pg

    def page_idx(r):
        def idx(b, p, pt):
            return (pool_offset + pt[b * n_pages + p * n_pg + r], 0, 0)
        return idx

    seq_idx = lambda b, p, pt: (b, 0, 0)
    head_idx = lambda b, p, pt: (b, 0, 0, 0)
    const2 = lambda b, p, pt: (0, 0)
    in_specs = [
        pl.BlockSpec(lam_p.shape, const2),
        pl.BlockSpec((1, HEAD_W), const2),
        pl.BlockSpec((1, N_HEADS, dl, HEAD_W), head_idx),
        pl.BlockSpec((1, rows, HEAD_W), seq_idx),
        pl.BlockSpec((1, rows, HEAD_W), seq_idx),
    ]
    in_specs += [pl.BlockSpec((1, page_rows, HEAD_W), page_idx(r)) for r in range(n_pg)]
    in_specs += [pl.BlockSpec((1, page_rows, HEAD_W), page_idx(r)) for r in range(n_pg)]
    grid_spec = pltpu.PrefetchScalarGridSpec(
        num_scalar_prefetch=1,
        grid=(DB, steps),
        in_specs=in_specs,
        out_specs=pl.BlockSpec((1, N_HEADS, dl, HEAD_W), head_idx),
        scratch_shapes=[
            pltpu.VMEM((N_HEADS, 2 * dl, HEAD_W), BF16),
            pltpu.VMEM((N_HEADS, 2 * dl, 1), F32),
            pltpu.VMEM((N_HEADS, 2 * dl, 1), F32),
            pltpu.VMEM((N_HEADS, 2 * dl, HEAD_W), F32),
        ],
    )
    kern = functools.partial(_paged_kernel, n_pg=n_pg, lam_init=lam_init)
    return pl.pallas_call(
        kern,
        grid_spec=grid_spec,
        out_shape=jax.ShapeDtypeStruct((DB, N_HEADS, dl, HEAD_W), F32),
        compiler_params=_cparams(("arbitrary", "arbitrary")),
        name="paged_diff_attention",
    )(page_table.reshape(-1), lam_p, g_subln, q_heads, k_new, v_new,
      *([cache_k] * n_pg), *([cache_v] * n_pg))


def _seg_scan(x, pos, seg, op, ident):
    s = 1
    while s < seg:
        shifted = pltpu.roll(x, s, 0)
        x = op(x, jnp.where(pos >= s, shifted, ident))
        s *= 2
    return x


def _seg_last(x, n_seg, seg):
    r, w = x.shape
    x3 = x.reshape(n_seg, seg, w)
    return jnp.broadcast_to(x3[:, seg - 1:seg, :], (n_seg, seg, w)).reshape(r, w)


def _mlstm_kernel(*refs, n_seg, seg, has_state):
    if has_state:
        (qk_ref, v_ref, o_ref, gt_ref, bg_ref, gn_ref, c_in, n_in, m_in,
         h_ref, c_out, n_out, m_out) = refs
    else:
        qk_ref, v_ref, o_ref, gt_ref, bg_ref, gn_ref, h_ref, c_out, n_out, m_out = refs
        c_in, n_in, m_in = c_out, n_out, m_out

        @pl.when(pl.program_id(1) == 0)
        def _():
            c_out[...] = jnp.zeros(c_out.shape, F32)
            n_out[...] = jnp.zeros(n_out.shape, F32)
            m_out[...] = jnp.zeros(m_out.shape, F32)

    R = n_seg * seg
    D = qk_ref.shape[1]
    KOFF = D // 2
    QK = ML_QK_DIM

    pre = gt_ref[...] + bg_ref[...]
    pre = GATE_CAP * jnp.tanh(pre * (1.0 / GATE_CAP))
    li = pre[:, :LANES]
    xf = pre[:, LANES:]
    lf = jnp.minimum(xf, 0.0) - jnp.log1p(jnp.exp(-jnp.abs(xf)))

    rowi = lax.broadcasted_iota(jnp.int32, (R, LANES), 0)
    pos = rowi & (seg - 1)
    bcum = _seg_scan(lf, pos, seg, jnp.add, 0.0)
    a = li - bcum
    cm = _seg_scan(a, pos, seg, jnp.maximum, NEG)
    m0 = m_in[...]
    mx = jnp.maximum(m0, cm)
    mt = bcum + mx
    u = -mx
    w_inter = jnp.exp(m0 + u)
    emt = jnp.exp(-mt)
    b_last = _seg_last(bcum, n_seg, seg)
    m_new = _seg_last(mt, n_seg, seg)
    decay = jnp.exp(b_last + m0 - m_new)
    w_state = jnp.exp(a + b_last - m_new)
    a_t = a.T

    tr = lax.broadcasted_iota(jnp.int32, (R, R), 0)
    tc = lax.broadcasted_iota(jnp.int32, (R, R), 1)
    if n_seg == 1:
        intra = tc <= tr
    else:
        shift = seg.bit_length() - 1
        intra = ((tr >> shift) == (tc >> shift)) & (tc <= tr)

    lane = lax.broadcasted_iota(jnp.int32, (R, LANES), 1)
    low = lane < QK
    hr = lax.broadcasted_iota(jnp.int32, (LANES, LANES), 0)
    hc = lax.broadcasted_iota(jnp.int32, (LANES, LANES), 1)
    half_sel = jnp.where(hc == jnp.where(hr < QK, 0, 1), 1.0, 0.0).astype(BF16)
    low_row = low[0:1]
    if n_seg > 1:
        colseg = lax.broadcasted_iota(jnp.int32, (QK, R), 1) >> (seg.bit_length() - 1)

    for p in range(N_HEADS // 2):
        ha, hb = 2 * p, 2 * p + 1
        qp = qk_ref[:, p * LANES:(p + 1) * LANES].astype(F32)
        kp_bf = qk_ref[:, KOFF + p * LANES:KOFF + (p + 1) * LANES]
        q_heads = (jnp.where(low, qp, 0.0).astype(BF16), jnp.where(low, 0.0, qp).astype(BF16))
        kw = kp_bf.astype(F32) * jnp.where(low, w_state[:, ha:ha + 1], w_state[:, hb:hb + 1])
        kw_t = kw.T
        if n_seg == 1:
            n_rows = n_in[0, p:p + 1, :]
        else:
            n_rows = jnp.concatenate(
                [jnp.broadcast_to(n_in[g, p:p + 1, :], (seg, LANES)) for g in range(n_seg)], axis=0)
        qn = jnp.dot((qp * n_rows).astype(BF16), half_sel, preferred_element_type=F32)
        sp = [c_in[g, ha:hb + 1].reshape(2 * QK, LANES).astype(BF16) for g in range(n_seg)]
        for hi, h in enumerate((ha, hb)):
            qh = q_heads[hi]
            s = lax.dot_general(qh, kp_bf, _NT, preferred_element_type=F32)
            wi = jnp.where(intra, jnp.exp(u[:, h:h + 1] + a_t[h:h + 1, :]), 0.0)
            sw = wi * s
            vh = v_ref[:, h * LANES:(h + 1) * LANES]
            num = jnp.dot(sw.astype(BF16), vh, preferred_element_type=F32)
            den = jnp.sum(sw, axis=-1, keepdims=True)
            if n_seg == 1:
                inter = jnp.dot(qh, sp[0], preferred_element_type=F32)
            else:
                inter = jnp.concatenate(
                    [jnp.dot(qh[g * seg:(g + 1) * seg], sp[g], preferred_element_type=F32)
                     for g in range(n_seg)], axis=0)
            wcol = w_inter[:, h:h + 1]
            num = num + inter * wcol
            den = den + wcol * qn[:, hi:hi + 1]
            denom = jnp.maximum(jnp.abs(den), emt[:, h:h + 1])
            hh = num / denom
            hn = _rmsnorm_rows(hh, gn_ref[:, h * LANES:(h + 1) * LANES])
            og = jax.nn.sigmoid(o_ref[:, h * LANES:(h + 1) * LANES])
            h_ref[:, h * LANES:(h + 1) * LANES] = (og * hn).astype(BF16)
            kw_h = kw_t[hi * QK:(hi + 1) * QK, :]
            if n_seg == 1:
                upd = jnp.dot(kw_h.astype(BF16), vh, preferred_element_type=F32)
                c_out[0, h] = decay[0:1, h:h + 1] * c_in[0, h] + upd
            else:
                lhs = jnp.concatenate(
                    [jnp.where(colseg == g, kw_h, 0.0) for g in range(n_seg)], axis=0).astype(BF16)
                upd = jnp.dot(lhs, vh, preferred_element_type=F32)
                for g in range(n_seg):
                    c_out[g, h] = decay[g * seg:g * seg + 1, h:h + 1] * c_in[g, h] + upd[g * QK:(g + 1) * QK]
        for g in range(n_seg):
            r0 = g * seg
            ksum = jnp.sum(kw[r0:r0 + seg], axis=0, keepdims=True)
            dpair = jnp.where(low_row, decay[r0:r0 + 1, ha:ha + 1], decay[r0:r0 + 1, hb:hb + 1])
            n_out[g, p:p + 1, :] = dpair * n_in[g, p:p + 1, :] + ksum
    m_out[...] = m_new


def mlstm_prompt(qk, v, o, gates, bgate, gnorm, batch, seq):
    T, D = qk.shape
    R = LANES
    nc = seq // R
    GW = gates.shape[1]
    row = lambda b, c: (b * nc + c, 0)
    const = lambda b, c: (0, 0)
    st = lambda b, c: (b, 0, 0, 0)
    kern = functools.partial(_mlstm_kernel, n_seg=1, seg=R, has_state=False)
    return pl.pallas_call(
        kern,
        grid=(batch, nc),
        in_specs=[
            pl.BlockSpec((R, D), row), pl.BlockSpec((R, D), row), pl.BlockSpec((R, D), row),
            pl.BlockSpec((R, GW), row), pl.BlockSpec((1, GW), const), pl.BlockSpec((1, D), const),
        ],
        out_specs=[
            pl.BlockSpec((R, D), row),
            pl.BlockSpec((1, N_HEADS, ML_QK_DIM, LANES), st),
            pl.BlockSpec((1, N_HEADS // 2, LANES), lambda b, c: (b, 0, 0)),
            pl.BlockSpec((R, LANES), lambda b, c: (b, 0)),
        ],
        out_shape=(jax.ShapeDtypeStruct((T, D), BF16),
                   jax.ShapeDtypeStruct((batch, N_HEADS, ML_QK_DIM, LANES), F32),
                   jax.ShapeDtypeStruct((batch, N_HEADS // 2, LANES), F32),
                   jax.ShapeDtypeStruct((batch * R, LANES), F32)),
        compiler_params=_cparams(("arbitrary", "arbitrary")),
        name="mlstm_prompt",
    )(qk, v, o, gates, bgate, gnorm)


def mlstm_sample(qk, v, o, gates, bgate, gnorm, c_all, layer, n0_pairs, m0_rows, seg):
    T, D = qk.shape
    R = LANES
    n_seg = R // seg
    GW = gates.shape[1]
    n_steps = T // R
    row = lambda i: (i, 0)
    const = lambda i: (0, 0)
    st = lambda i: (i, 0, 0, 0)
    st_in = lambda i: (layer * n_steps + i, 0, 0, 0)
    st_block = (n_seg, N_HEADS, ML_QK_DIM, LANES)
    n_block = (n_seg, N_HEADS // 2, LANES)
    nst = lambda i: (i, 0, 0)
    kern = functools.partial(_mlstm_kernel, n_seg=n_seg, seg=seg, has_state=True)
    return pl.pallas_call(
        kern,
        grid=(T // R,),
        in_specs=[
            pl.BlockSpec((R, D), row), pl.BlockSpec((R, D), row), pl.BlockSpec((R, D), row),
            pl.BlockSpec((R, GW), row), pl.BlockSpec((1, GW), const), pl.BlockSpec((1, D), const),
            pl.BlockSpec(st_block, st_in), pl.BlockSpec(n_block, nst), pl.BlockSpec((R, LANES), row),
        ],
        out_specs=[
            pl.BlockSpec((R, D), row),
            pl.BlockSpec(st_block, st),
            pl.BlockSpec(n_block, nst),
            pl.BlockSpec((R, LANES), row),
        ],
        out_shape=(jax.ShapeDtypeStruct((T, D), BF16),
                   jax.ShapeDtypeStruct((n_steps * n_seg,) + st_block[1:], F32),
                   jax.ShapeDtypeStruct(n0_pairs.shape, F32),
                   jax.ShapeDtypeStruct((T, LANES), F32)),
        compiler_params=_cparams(("arbitrary",)),
        name="mlstm_sample",
    )(qk, v, o, gates, bgate, gnorm, c_all, n0_pairs, m0_rows)


def _ffn_core(x_ref, a_ref, wo_ref, gn_ref, wg_ref, wu_ref, wd_ref, cw_ref, y_ref, hn_ref, shifted,
              final_g_ref):
    f = pl.program_id(1)

    @pl.when(f == 0)
    def _():
        xm = x_ref[...] + jnp.dot(a_ref[...], wo_ref[...], preferred_element_type=F32)
        y_ref[...] = xm
        hn_ref[...] = _rmsnorm_rows(xm, gn_ref[...]).astype(BF16)

    hn = hn_ref[...]
    g = jnp.dot(hn, wg_ref[...], preferred_element_type=F32)
    u = jnp.dot(hn, wu_ref[...], preferred_element_type=F32)
    g1, g2 = shifted(g)
    cw = cw_ref[...]
    gc = cw[3:4] + cw[0:1] * g2 + cw[1:2] * g1 + cw[2:3] * g
    act = (jax.nn.silu(gc) * u).astype(BF16)
    y_ref[...] += jnp.dot(act, wd_ref[...], preferred_element_type=F32)

    if final_g_ref is not None:
        @pl.when(f == pl.num_programs(1) - 1)
        def _():
            y_ref[...] = _rmsnorm_rows(y_ref[...], final_g_ref[...])
    return g


def _ffn_prompt_kernel(*refs, tiles_per_seq, final_norm):
    if final_norm:
        (x_ref, a_ref, wo_ref, gn_ref, wg_ref, wu_ref, wd_ref, cw_ref, fg_ref,
         y_ref, st_ref, hn_ref, gbuf, carry) = refs
    else:
        x_ref, a_ref, wo_ref, gn_ref, wg_ref, wu_ref, wd_ref, cw_ref, y_ref, st_ref, hn_ref, gbuf, carry = refs
        fg_ref = None
    i = pl.program_id(0)
    f = pl.program_id(1)
    tm = x_ref.shape[0]

    @pl.when((i == 0) & (f == 0))
    def _():
        carry[...] = jnp.zeros(carry.shape, F32)

    def shifted(g):
        prev = jnp.where(i % tiles_per_seq == 0, 0.0, carry[f])
        gbuf[0:SUBLANES] = prev
        gbuf[SUBLANES:] = g
        return gbuf[SUBLANES - 1:SUBLANES - 1 + tm], gbuf[SUBLANES - 2:SUBLANES - 2 + tm]

    g = _ffn_core(x_ref, a_ref, wo_ref, gn_ref, wg_ref, wu_ref, wd_ref, cw_ref, y_ref, hn_ref, shifted, fg_ref)
    tail = g[tm - SUBLANES:]
    carry[f] = tail
    st_ref[0] = tail


def _ffn_sample_kernel(*refs, seg, final_norm):
    if final_norm:
        x_ref, a_ref, wo_ref, gn_ref, wg_ref, wu_ref, wd_ref, cw_ref, e_ref, fg_ref, y_ref, g_ref, hn_ref = refs
    else:
        x_ref, a_ref, wo_ref, gn_ref, wg_ref, wu_ref, wd_ref, cw_ref, e_ref, y_ref, g_ref, hn_ref = refs
        fg_ref = None
    tm = x_ref.shape[0]

    def shifted(g):
        e = e_ref[...]
        pos = lax.broadcasted_iota(jnp.int32, g.shape, 0) & (seg - 1)
        g1 = jnp.where(pos < 1, pltpu.roll(e, tm - 1, 0), pltpu.roll(g, 1, 0))
        g2 = jnp.where(pos < 2, e, pltpu.roll(g, 2, 0))
        return g1, g2

    g = _ffn_core(x_ref, a_ref, wo_ref, gn_ref, wg_ref, wu_ref, wd_ref, cw_ref, y_ref, hn_ref, shifted, fg_ref)
    g_ref[...] = g


def _ffn_common_specs(D, tm, tf, n_f):
    return [
        pl.BlockSpec((tm, D), lambda i, f: (i, 0)),
        pl.BlockSpec((tm, D), lambda i, f: (i, 0)),
        pl.BlockSpec((D, D), lambda i, f: (0, 0)),
        pl.BlockSpec((1, D), lambda i, f: (0, 0)),
        pl.BlockSpec((D, tf), lambda i, f: (0, f)),
        pl.BlockSpec((D, tf), lambda i, f: (0, n_f + f)),
        pl.BlockSpec((tf, D), lambda i, f: (f, 0)),
        pl.BlockSpec((SUBLANES, tf), lambda i, f: (0, f)),
    ]


def ffn_prompt(x, a_bf, w_o_bf, gn, w_up_bf, w_down_bf, cw, batch, seq, tm, tf, final_g=None):
    T, D = x.shape
    F = w_down_bf.shape[0]
    n_f = F // tf
    tps = seq // tm
    in_specs = _ffn_common_specs(D, tm, tf, n_f)
    args = [x, a_bf, w_o_bf, gn, w_up_bf, w_up_bf, w_down_bf, cw]
    if final_g is not None:
        in_specs.append(pl.BlockSpec((1, D), lambda i, f: (0, 0)))
        args.append(final_g)
    kern = functools.partial(_ffn_prompt_kernel, tiles_per_seq=tps, final_norm=final_g is not None)
    return pl.pallas_call(
        kern,
        grid=(T // tm, n_f),
        in_specs=in_specs,
        out_specs=[
            pl.BlockSpec((tm, D), lambda i, f: (i, 0)),
            pl.BlockSpec((1, SUBLANES, tf), lambda i, f: (i, 0, f)),
        ],
        out_shape=(jax.ShapeDtypeStruct((T, D), F32),
                   jax.ShapeDtypeStruct((T // tm, SUBLANES, F), F32)),
        scratch_shapes=[
            pltpu.VMEM((tm, D), BF16),
            pltpu.VMEM((tm + SUBLANES, tf), F32),
            pltpu.VMEM((n_f, SUBLANES, tf), F32),
        ],
        compiler_params=_cparams(("arbitrary", "arbitrary")),
        name="ffn_prompt",
    )(*args)


def ffn_sample(x, a_bf, w_o_bf, gn, w_up_bf, w_down_bf, cw, e_rows, seg, tm, tf, final_g=None):
    T, D = x.shape
    F = w_down_bf.shape[0]
    n_f = F // tf
    in_specs = _ffn_common_specs(D, tm, tf, n_f)
    in_specs.append(pl.BlockSpec((tm, tf), lambda i, f: (i, f)))
    args = [x, a_bf, w_o_bf, gn, w_up_bf, w_up_bf, w_down_bf, cw, e_rows]
    if final_g is not None:
        in_specs.append(pl.BlockSpec((1, D), lambda i, f: (0, 0)))
        args.append(final_g)
    kern = functools.partial(_ffn_sample_kernel, seg=seg, final_norm=final_g is not None)
    return pl.pallas_call(
        kern,
        grid=(T // tm, n_f),
        in_specs=in_specs,
        out_specs=[
            pl.BlockSpec((tm, D), lambda i, f: (i, 0)),
            pl.BlockSpec((tm, tf), lambda i, f: (i, f)),
        ],
        out_shape=(jax.ShapeDtypeStruct((T, D), F32),
                   jax.ShapeDtypeStruct((T, F), F32)),
        scratch_shapes=[pltpu.VMEM((tm, D), BF16)],
        compiler_params=_cparams(("arbitrary", "arbitrary")),
        name="ffn_sample",
    )(*args)


def kernel(x_prompt, x_sample, cache_k, cache_v, page_table, state_mlstm_c, state_mlstm_n, state_mlstm_m,
           state_conv, g_mix_norm, g_ffn_norm, g_final, attn_w_qkv, attn_lambda, attn_subln, attn_w_o,
           ml_w_in, ml_b_gate, ml_norm, ml_w_out, ffn_w_up, ffn_conv_w, ffn_conv_b, ffn_w_down):
    B, L, D = x_prompt.shape
    DB, DL, _ = x_sample.shape
    depth = g_mix_norm.shape[0]
    n_attn, n_pool = cache_k.shape[:2]
    n_pages = page_table.shape[1]
    past = n_pages * PAGE_SIZE
    d_ff = ffn_w_down.shape[1]
    n_ml_heads = state_mlstm_c.shape[2]
    assert D == N_HEADS * HEAD_W and n_ml_heads == N_HEADS and DL == SUBLANES
    assert cache_k.shape[2:] == (PAGE_SIZE, N_HEADS, HEAD_W) and cache_v.shape[2:] == (PAGE_SIZE, N_HEADS, HEAD_W)

    TP, TS = B * L, DB * DL
    TM = 512
    TM_S = min(TM, TS)
    TM_FFN = 1024
    TM_FFN_S = min(TM_FFN, TS)
    TF = 256
    TQ = 512
    N_PG = min(16, n_pages)

    xp = x_prompt.reshape(TP, D)
    xs = x_sample.reshape(TS, D)
    rope_p, rope_s = rope_tables(L, past, DL, TM_S)
    c_all = state_mlstm_c.reshape((-1,) + state_mlstm_c.shape[2:])
    ck = cache_k.reshape(n_attn * n_pool, PAGE_SIZE * N_HEADS, HEAD_W)
    cv = cache_v.reshape(n_attn * n_pool, PAGE_SIZE * N_HEADS, HEAD_W)

    qk_w = N_HEADS * ML_QK_DIM
    k_p, v_p, k_s, v_s = [], [], [], []
    c_p, n_p, m_p, c_s, n_s, m_s = [], [], [], [], [], []
    cv_p, cv_s = [], []
    for i in range(depth):
        j = i // 2
        g_mix = g_mix_norm[i].reshape(1, D)
        if i % 2 == 0:
            lam_init = 0.8 - 0.6 * math.exp(-0.3 * i)
            w_qkv = attn_w_qkv[j].astype(BF16)
            w_o = attn_w_o[j].astype(BF16)
            g_sub = attn_subln[j].reshape(1, HEAD_W)
            lam_p = attn_lambda[j]
            qp, kbp, vbp, kp, vp = qkv_proj(xp, g_mix, w_qkv, rope_p, TM, L // TM)
            qs, _, _, ks, vs = qkv_proj(xs, g_mix, w_qkv, rope_s, TM_S, 1)
            ap = flash_diff_attention(qp, kbp, vbp, lam_p, g_sub, B, L, lam_init, TQ)
            rows = DL * N_HEADS
            q_heads = jnp.transpose(qs.reshape(DB, DL, N_HEADS, HEAD_W), (0, 2, 1, 3)).astype(F32)
            as_ = paged_diff_attention(
                q_heads, ks.reshape(DB, rows, HEAD_W), vs.reshape(DB, rows, HEAD_W),
                ck, cv, page_table, j * n_pool, lam_p, g_sub, lam_init, N_PG)
            as_ = jnp.transpose(as_, (0, 2, 1, 3)).reshape(TS, D).astype(BF16)
            mix_p, mix_s, w_mix = ap, as_, w_o
            k_p.append(kp.reshape(B, L, N_HEADS, HEAD_W))
            v_p.append(vp.reshape(B, L, N_HEADS, HEAD_W))
            k_s.append(ks.reshape(DB, DL, N_HEADS, HEAD_W))
            v_s.append(vs.reshape(DB, DL, N_HEADS, HEAD_W))
        else:
            w_in = ml_w_in[j]
            w_main = w_in[:, :2 * qk_w + 2 * D].astype(BF16)
            w_gate = w_in[:, 2 * qk_w + 2 * D:]
            pad = jnp.zeros((D, LANES - N_HEADS), F32)
            wg = jnp.concatenate([w_gate[:, :N_HEADS], pad, w_gate[:, N_HEADS:], pad], axis=1).astype(BF16)
            bg = ml_b_gate[j]
            padb = jnp.zeros((LANES - N_HEADS,), F32)
            bgate = jnp.concatenate([bg[:N_HEADS], padb, bg[N_HEADS:], padb]).reshape(1, 2 * LANES)
            gnorm = ml_norm[j].reshape(1, D)
            w_out = ml_w_out[j].astype(BF16)
            qkp, vvp, oop, gtp = mlstm_in_proj(xp, g_mix, w_main, wg, TM)
            qks, vvs, oos, gts = mlstm_in_proj(xs, g_mix, w_main, wg, TM_S)
            hp, cp_, np_, mp_ = mlstm_prompt(qkp, vvp, oop, gtp, bgate, gnorm, B, L)
            n0_pairs = state_mlstm_n[j].reshape(DB, N_HEADS // 2, LANES)
            m0_rows = jnp.repeat(jnp.pad(state_mlstm_m[j], ((0, 0), (0, LANES - N_HEADS))), DL, axis=0)
            hs, cs_, ns_, ms_ = mlstm_sample(qks, vvs, oos, gts, bgate, gnorm,
                                             c_all, j, n0_pairs, m0_rows, DL)
            mix_p, mix_s, w_mix = hp, hs, w_out
            c_p.append(cp_)
            n_p.append(np_.reshape(B, N_HEADS, ML_QK_DIM))
            m_p.append(mp_.reshape(B, LANES, LANES)[:, 0, :N_HEADS])
            c_s.append(cs_)
            n_s.append(ns_.reshape(DB, N_HEADS, ML_QK_DIM))
            m_s.append(ms_.reshape(DB, DL, LANES)[:, DL - 1, :N_HEADS])
        g_ffn = g_ffn_norm[i].reshape(1, D)
        w_up = ffn_w_up[i].astype(BF16)
        w_down = ffn_w_down[i].astype(BF16)
        cw = jnp.concatenate([ffn_conv_w[i], ffn_conv_b[i][None, :],
                              jnp.zeros((SUBLANES - CONV_W - 1, d_ff), F32)], axis=0)
        fg = g_final.reshape(1, D) if i == depth - 1 else None
        xp, stp = ffn_prompt(xp, mix_p, w_mix, g_ffn, w_up, w_down, cw, B, L, TM_FFN, TF, fg)
        e_rows = jnp.pad(state_conv[i], ((0, 0), (0, DL - (CONV_W - 1)), (0, 0))).reshape(TS, d_ff)
        xs, gs = ffn_sample(xs, mix_s, w_mix, g_ffn, w_up, w_down, cw, e_rows, DL, TM_FFN_S, TF, fg)
        cv_p.append(stp.reshape(B, L // TM_FFN, SUBLANES, d_ff)[:, -1, SUBLANES - (CONV_W - 1):, :])
        cv_s.append(gs.reshape(DB, DL, d_ff)[:, DL - (CONV_W - 1):, :])

    return (xp.reshape(B, L, D), xs.reshape(DB, DL, D),
            jnp.stack(k_p), jnp.stack(v_p), jnp.stack(k_s), jnp.stack(v_s),
            jnp.stack(c_p), jnp.stack(n_p), jnp.stack(m_p),
            jnp.stack(c_s), jnp.stack(n_s), jnp.stack(m_s),
            jnp.stack(cv_p), jnp.stack(cv_s))
```
